```python
import math
import jax, jax.numpy as jnp
from jax import lax
import numpy as np

D_MODEL = 1024
BATCH = 8
SEQ = 2048
DEPTH = 1
DEC_BATCH = 128
DEC_SEQ = 1
PAST_LEN = 16384
PAGE_SIZE = 128

GLA_HEADS = 4
GLA_DK = 64
GLA_DV = 128
GLA_QK = GLA_HEADS * GLA_DK
GLA_V = GLA_HEADS * GLA_DV
GLA_RANK = 16
GLA_GATE_TEMP = 16.0
SSD_HEADS = 8
SSD_HEADDIM = 64
SSD_INNER = SSD_HEADS * SSD_HEADDIM
SSD_GROUPS = 2
SSD_STATE = 64
CONV_W = 4
SSD_CONV_DIM = SSD_INNER + 2 * SSD_GROUPS * SSD_STATE
CHUNK = 64
EPS = 1e-6
PROJ_SIZES = (GLA_QK, GLA_QK, GLA_V, GLA_V, GLA_RANK, SSD_INNER, SSD_CONV_DIM, SSD_HEADS, D_MODEL, D_MODEL)
IN_DIM = sum(PROJ_SIZES)

kernel_name = 'gla_ssd_gated_hybrid_step'


def rmsnorm(x, g):
    xf = x.astype(jnp.float32)
    y = xf * lax.rsqrt(jnp.mean(xf * xf, axis=-1, keepdims=True) + EPS)
    return (y * g.astype(jnp.float32)).astype(x.dtype)


def _split_points(sizes):
    return [int(v) for v in np.cumsum(sizes)[:-1]]


def _chunking(L):
    c = min(CHUNK, L)
    return c, -(-L // c)


def _to_chunks(a, c, n):
    L = a.shape[1]
    a = jnp.pad(a, [(0, 0), (0, n * c - L)] + [(0, 0)] * (a.ndim - 2))
    a = a.reshape(a.shape[0], n, c, *a.shape[2:])
    return jnp.moveaxis(a, 1, 0)


def _from_chunks(a, L):
    a = jnp.moveaxis(a, 0, 1)
    a = a.reshape(a.shape[0], a.shape[1] * a.shape[2], *a.shape[3:])
    return a[:, :L]


def gla_recurrence(q, k, v, log_a, S0):
    L = q.shape[1]
    c, n = _chunking(L)
    mask = jnp.tril(jnp.ones((c, c), dtype=bool))[None, :, :, None, None]

    def step(S, inp):
        qc, kc, vc, ac = inp
        b = jnp.cumsum(ac, axis=1)
        decay = jnp.exp(jnp.where(mask, b[:, :, None] - b[:, None, :], -jnp.inf))
        A = jnp.einsum('bihd,bjhd,bijhd->bhij', qc, kc, decay)
        o = jnp.einsum('bhij,bjhe->bihe', A, vc) + jnp.einsum('bihd,bhde->bihe', qc * jnp.exp(b), S)
        b_last = b[:, -1]
        S_new = jnp.exp(b_last)[..., None] * S + jnp.einsum(
            'bjhd,bjhe->bhde', kc * jnp.exp(b_last[:, None] - b), vc)
        return S_new, o

    xs = tuple(_to_chunks(a, c, n) for a in (q, k, v, log_a))
    S_fin, o = lax.scan(step, S0, xs)
    return _from_chunks(o, L), S_fin


def ssd_recurrence(x, dt, A, Bh, Ch, h0):
    L = x.shape[1]
    c, n = _chunking(L)
    mask = jnp.tril(jnp.ones((c, c), dtype=bool))[None, :, :, None]
    la = dt * A

    def step(h, inp):
        xc, dtc, lac, bc, cc = inp
        cum = jnp.cumsum(lac, axis=1)
        decay = jnp.exp(jnp.where(mask, cum[:, :, None] - cum[:, None, :], -jnp.inf))
        scores = jnp.einsum('bihn,bjhn->bijh', cc, bc) * decay
        xdt = xc * dtc[..., None]
        y = jnp.einsum('bijh,bjhp->bihp', scores, xdt) + jnp.einsum(
            'bihn,bhpn->bihp', cc * jnp.exp(cum)[..., None], h)
        c_last = cum[:, -1]
        h_new = jnp.exp(c_last)[:, :, None, None] * h + jnp.einsum(
            'bjhp,bjhn->bhpn', xdt * jnp.exp(c_last[:, None] - cum)[..., None], bc)
        return h_new, y

    xs = tuple(_to_chunks(a, c, n) for a in (x, dt, la, Bh, Ch))
    h_fin, y = lax.scan(step, h0, xs)
    return _from_chunks(y, L), h_fin


def causal_conv(u, buf, w, bias):
    full = jnp.concatenate([buf.astype(u.dtype), u], axis=1)
    out = lax.conv_general_dilated(full, w[:, None, :].astype(u.dtype), window_strides=(1,), padding='VALID',
                                   dimension_numbers=('NWC', 'WIO', 'NWC'), feature_group_count=u.shape[-1])
    return out + bias.astype(u.dtype), full[:, -(CONV_W - 1):]


def mixer_layer(x, S0, h0, buf, norm_g, w_in, w_a2, b_a2, gla_norm_g, conv_w, conv_b, dt_bias, a_log,
                d_skip, ssd_norm_g, w_br_gla, w_br_ssd, w_out):
    f32 = jnp.float32
    Bsz, L, _ = x.shape
    hn = rmsnorm(x, norm_g)
    proj = hn @ w_in
    q, k, v, g, a_lr, z, xbc, dt_raw, gate_a, gate_b = jnp.split(proj, _split_points(PROJ_SIZES), axis=-1)

    qh = q.reshape(Bsz, L, GLA_HEADS, GLA_DK).astype(f32) * (GLA_DK ** -0.5)
    kh = k.reshape(Bsz, L, GLA_HEADS, GLA_DK).astype(f32)
    vh = v.reshape(Bsz, L, GLA_HEADS, GLA_DV).astype(f32)
    log_a = (jax.nn.log_sigmoid((a_lr @ w_a2 + b_a2).astype(f32)) / GLA_GATE_TEMP).reshape(Bsz, L, GLA_HEADS, GLA_DK)
    o, S_new = gla_recurrence(qh, kh, vh, log_a, S0.astype(f32))
    o = rmsnorm(o, gla_norm_g.reshape(GLA_HEADS, GLA_DV)).reshape(Bsz, L, GLA_V)
    o = (o * jax.nn.silu(g.astype(f32))).astype(x.dtype)

    xbc_c, buf_new = causal_conv(xbc, buf, conv_w, conv_b)
    xbc_c = jax.nn.silu(xbc_c.astype(f32))
    xs, Bm, Cm = jnp.split(xbc_c, [SSD_INNER, SSD_INNER + SSD_GROUPS * SSD_STATE], axis=-1)
    dt = jax.nn.softplus(dt_raw.astype(f32) + dt_bias.astype(f32))
    A = -jnp.exp(a_log.astype(f32))
    xs_h = xs.reshape(Bsz, L, SSD_HEADS, SSD_HEADDIM)
    rep = SSD_HEADS // SSD_GROUPS
    Bh = jnp.repeat(Bm.reshape(Bsz, L, SSD_GROUPS, SSD_STATE), rep, axis=2)
    Ch = jnp.repeat(Cm.reshape(Bsz, L, SSD_GROUPS, SSD_STATE), rep, axis=2)
    y, h_new = ssd_recurrence(xs_h, dt, A, Bh, Ch, h0.astype(f32))
    y = y + d_skip.astype(f32)[:, None] * xs_h
    y = y.reshape(Bsz, L, SSD_INNER) * jax.nn.silu(z.astype(f32))
    y = rmsnorm(y.reshape(Bsz, L, SSD_GROUPS, SSD_INNER // SSD_GROUPS),
                ssd_norm_g.reshape(SSD_GROUPS, SSD_INNER // SSD_GROUPS)).reshape(Bsz, L, SSD_INNER).astype(x.dtype)

    merged = jax.nn.sigmoid(gate_a) * (o @ w_br_gla) + jax.nn.sigmoid(gate_b) * (y @ w_br_ssd)
    out = x + merged @ w_out
    return out, S_new.astype(x.dtype), h_new.astype(x.dtype), buf_new.astype(x.dtype)


def trunk(x, s_gla, s_ssm, s_conv, params, final_norm_g):
    new_g, new_s, new_c = [], [], []
    for l in range(DEPTH):
        x, sg, ss, sc = mixer_layer(x, s_gla[l], s_ssm[l], s_conv[l], *[p[l] for p in params])
        new_g.append(sg)
        new_s.append(ss)
        new_c.append(sc)
    return rmsnorm(x, final_norm_g), jnp.stack(new_g), jnp.stack(new_s), jnp.stack(new_c)


def setup_inputs(seed: int = 0) -> dict:
    key = jax.random.key(seed)
    ks = jax.random.split(key, 20)
    f = jnp.float32

    def nrm(k, shape, s):
        return jax.random.normal(k, shape, f) * s

    dt0 = jnp.exp(jax.random.uniform(ks[12], (DEPTH, SSD_HEADS), f, math.log(1e-3), math.log(1e-1)))
    return {
        'x_prompt': nrm(ks[0], (BATCH, SEQ, D_MODEL), 1.0),
        'x_sample': nrm(ks[1], (DEC_BATCH, DEC_SEQ, D_MODEL), 1.0),
        'state_gla': nrm(ks[2], (DEPTH, DEC_BATCH, GLA_HEADS, GLA_DK, GLA_DV), 1.0),
        'state_ssm': nrm(ks[3], (DEPTH, DEC_BATCH, SSD_HEADS, SSD_HEADDIM, SSD_STATE), 0.5),
        'state_conv': nrm(ks[4], (DEPTH, DEC_BATCH, CONV_W - 1, SSD_CONV_DIM), 1.0),
        'norm_g': 1.0 + nrm(ks[5], (DEPTH, D_MODEL), 0.02),
        'w_in': nrm(ks[6], (DEPTH, D_MODEL, IN_DIM), D_MODEL ** -0.5),
        'w_a2': nrm(ks[7], (DEPTH, GLA_RANK, GLA_QK), GLA_RANK ** -0.5),
        'b_a2': nrm(ks[8], (DEPTH, GLA_QK), 0.1),
        'gla_norm_g': 1.0 + nrm(ks[9], (DEPTH, GLA_V), 0.02),
        'conv_w': nrm(ks[10], (DEPTH, CONV_W, SSD_CONV_DIM), CONV_W ** -0.5),
        'conv_b': nrm(ks[11], (DEPTH, SSD_CONV_DIM), 0.01),
        'dt_bias': dt0 + jnp.log(-jnp.expm1(-dt0)),
        'a_log': jnp.log(jax.random.uniform(ks[13], (DEPTH, SSD_HEADS), f, 1.0, 16.0)),
        'd_skip': 1.0 + nrm(ks[14], (DEPTH, SSD_HEADS), 0.1),
        'ssd_norm_g': 1.0 + nrm(ks[15], (DEPTH, SSD_INNER), 0.02),
        'w_br_gla': nrm(ks[16], (DEPTH, GLA_V, D_MODEL), GLA_V ** -0.5),
        'w_br_ssd': nrm(ks[17], (DEPTH, SSD_INNER, D_MODEL), SSD_INNER ** -0.5),
        'w_out': nrm(ks[18], (DEPTH, D_MODEL, D_MODEL), D_MODEL ** -0.5),
        'final_norm_g': 1.0 + nrm(ks[19], (D_MODEL,), 0.02),
    }


def reference(x_prompt, x_sample, state_gla, state_ssm, state_conv, norm_g, w_in, w_a2, b_a2, gla_norm_g,
              conv_w, conv_b, dt_bias, a_log, d_skip, ssd_norm_g, w_br_gla, w_br_ssd, w_out, final_norm_g):
    params = (norm_g, w_in, w_a2, b_a2, gla_norm_g, conv_w, conv_b, dt_bias, a_log, d_skip, ssd_norm_g,
              w_br_gla, w_br_ssd, w_out)
    bp = x_prompt.shape[0]
    dtp = x_prompt.dtype
    zero_gla = jnp.zeros((DEPTH, bp, GLA_HEADS, GLA_DK, GLA_DV), dtp)
    zero_ssm = jnp.zeros((DEPTH, bp, SSD_HEADS, SSD_HEADDIM, SSD_STATE), dtp)
    zero_conv = jnp.zeros((DEPTH, bp, CONV_W - 1, SSD_CONV_DIM), dtp)
    y_prompt, gla_p, ssm_p, conv_p = trunk(x_prompt, zero_gla, zero_ssm, zero_conv, params, final_norm_g)
    y_sample, gla_s, ssm_s, conv_s = trunk(x_sample, state_gla, state_ssm, state_conv, params, final_norm_g)
    return (y_prompt, y_sample, gla_p, ssm_p, conv_p, gla_s, ssm_s, conv_s)
```

```python
import functools

import numpy as np
import jax
import jax.numpy as jnp
from jax import lax
from jax.experimental import pallas as pl
from jax.experimental.pallas import tpu as pltpu

F32 = jnp.float32
BF16 = jnp.bfloat16

D_MODEL = 1024
GLA_HEADS = 4
GLA_DK = 64
GLA_DV = 128
GLA_QK = GLA_HEADS * GLA_DK
GLA_V = GLA_HEADS * GLA_DV
GLA_RANK = 16
GLA_GATE_TEMP = 16.0
SSD_HEADS = 8
SSD_HEADDIM = 64
SSD_INNER = SSD_HEADS * SSD_HEADDIM
SSD_GROUPS = 2
SSD_STATE = 64
SSD_BC = SSD_GROUPS * SSD_STATE
CONV_W = 4
SSD_CONV_DIM = SSD_INNER + 2 * SSD_BC
CHUNK = 64
EPS = 1e-6
PROJ_SIZES = (GLA_QK, GLA_QK, GLA_V, GLA_V, GLA_RANK, SSD_INNER, SSD_CONV_DIM, SSD_HEADS, D_MODEL, D_MODEL)

LANES = 128
SUBLANES = 8

C_Q = 0
C_K = C_Q + GLA_QK
C_V = C_K + GLA_QK
C_G = C_V + GLA_V
C_Z = C_G + GLA_V
C_XBC = C_Z + SSD_INNER
C_GA = C_XBC + SSD_CONV_DIM
C_GB = C_GA + D_MODEL
C_MISC = C_GB + D_MODEL
N_PROJ = C_MISC + LANES
MISC_DT = GLA_RANK
PROJ_NB = 512

PROMPT_T = 256
SAMPLE_SB = 8
VMEM_LIMIT = 56 * 1024 * 1024


def _split3(x):
    hi = x.astype(BF16)
    r1 = x - hi.astype(F32)
    mid = r1.astype(BF16)
    lo = (r1 - mid.astype(F32)).astype(BF16)
    return hi, mid, lo


def _sigmoid(x):
    return 1.0 / (1.0 + jnp.exp(-x))


def _softplus(x):
    return jnp.maximum(x, 0.0) + jnp.log1p(jnp.exp(-jnp.abs(x)))


def _rms_scale(x):
    return lax.rsqrt(jnp.mean(x * x, axis=-1, keepdims=True) + EPS)


def _dot(a, b):
    return jnp.dot(a, b, preferred_element_type=F32)


def _dot_nt(a, b):
    return lax.dot_general(a, b, (((1,), (1,)), ((), ())), preferred_element_type=F32)


def _dot_tn(a, b):
    return lax.dot_general(a, b, (((0,), (0,)), ((), ())), preferred_element_type=F32)


def _consts():
    r = np.arange

    def m(a, dt):
        return jnp.asarray(a.astype(np.float32), dtype=dt)

    c = CHUNK
    return dict(
        tril3=m(r(c)[:, None] >= (r(3 * c)[None, :] % c), BF16),
        km=m(r(GLA_HEADS * c)[:, None] // c == r(GLA_QK)[None, :] // GLA_DK, BF16),
        tril4=m(r(c)[:, None] >= r(GLA_HEADS * c)[None, :] % c, F32),
        vm=m(r(GLA_HEADS * c)[:, None] // c == r(GLA_V)[None, :] // GLA_DV, BF16),
        stm=m(r(GLA_V)[:, None] // GLA_DV == r(GLA_QK)[None, :] // GLA_DK, F32),
        bm=m(r(SSD_HEADS * c)[:, None] // (c * SSD_HEADS // SSD_GROUPS) == r(SSD_BC)[None, :] // SSD_STATE, BF16),
        tril8=m(r(c)[:, None] >= r(SSD_HEADS * c)[None, :] % c, F32),
        diag8=m(r(c)[:, None] == r(SSD_HEADS * c)[None, :] % c, F32),
        xm=m(r(2 * c)[:, None] // c == r(2 * SSD_HEADDIM)[None, :] // SSD_HEADDIM, BF16),
        hm=m(r(SSD_BC)[:, None] // SSD_STATE == r(SSD_INNER)[None, :] // (SSD_INNER // SSD_GROUPS), F32),
        e3=m((r(3 * LANES)[:, None] % LANES) == MISC_DT + r(SSD_INNER)[None, :] // SSD_HEADDIM, BF16),
    )


_CONST_ORDER = ("tril3", "km", "tril4", "vm", "stm", "bm", "tril8", "diag8", "xm", "hm", "e3")


def _pack_params(norm_g, w_in, w_a2, b_a2, gla_norm_g, conv_w, conv_b, dt_bias, a_log, d_skip, ssd_norm_g,
                 w_br_gla, w_br_ssd, w_out, final_norm_g):
    pts = [int(v) for v in np.cumsum(PROJ_SIZES)[:-1]]
    q, k, v, g, a_lr, z, xbc, dt, ga, gb = jnp.split(w_in, pts, axis=-1)
    misc = jnp.concatenate([a_lr, dt, jnp.zeros((D_MODEL, LANES - GLA_RANK - SSD_HEADS), w_in.dtype)], axis=-1)
    w = jnp.concatenate([q, k, v, g, z, xbc, ga, gb, misc], axis=-1).astype(BF16)
    wa2 = jnp.concatenate([w_a2, jnp.zeros((LANES - GLA_RANK, GLA_QK), w_a2.dtype)], axis=0).astype(BF16)

    def misc_row(p):
        return jnp.zeros((1, LANES), F32).at[0, MISC_DT:MISC_DT + SSD_HEADS].set(p.astype(F32))

    return dict(
        w=w, wa2=wa2, wbg=w_br_gla.astype(BF16), wbs=w_br_ssd.astype(BF16), wout=w_out.astype(BF16),
        ng=norm_g.reshape(1, D_MODEL), ba2=b_a2.reshape(1, GLA_QK), glag=gla_norm_g.reshape(1, GLA_V),
        cw=conv_w, cb=conv_b.reshape(1, SSD_CONV_DIM), dtb=misc_row(dt_bias), alog=misc_row(a_log),
        dsk=jnp.repeat(d_skip.astype(F32), SSD_HEADDIM).reshape(1, SSD_INNER),
        ssdg=ssd_norm_g.reshape(1, SSD_INNER), fg=final_norm_g.reshape(1, D_MODEL),
    )


_PARAM_ORDER = ("w", "wa2", "wbg", "wbs", "wout", "ng", "ba2", "glag", "cw", "cb", "dtb", "alog", "dsk", "ssdg", "fg")


def _a_pad(alog_ref):
    lane = lax.broadcasted_iota(jnp.int32, (1, LANES), 1)
    is_dt = (lane >= MISC_DT) & (lane < MISC_DT + SSD_HEADS)
    return jnp.where(is_dt, -jnp.exp(alog_ref[...]), 0.0)


def _project(hn_ref, w_ref, p_ref, row0, nrows):
    for n0 in range(0, N_PROJ, PROJ_NB):
        n1 = min(n0 + PROJ_NB, N_PROJ)
        p_ref[row0:row0 + nrows, n0:n1] = _dot(hn_ref[...], w_ref[:, n0:n1])


def _merge_out(x, o_bf, y_bf, gate_a, gate_b, wbg_ref, wbs_ref, wout_ref, fg_ref):
    merged = _sigmoid(gate_a) * _dot(o_bf, wbg_ref[...]) + _sigmoid(gate_b) * _dot(y_bf, wbs_ref[...])
    out = x + _dot(merged.astype(BF16), wout_ref[...])
    return out * _rms_scale(out) * fg_ref[...]


def _head_rmsnorm(x, width):
    parts = []
    for c0 in range(0, x.shape[-1], width):
        blk = x[:, c0:c0 + width]
        parts.append(blk * _rms_scale(blk))
    return jnp.concatenate(parts, axis=-1)


def _prompt_kernel(T, x_ref, w_ref, wa2_ref, wbg_ref, wbs_ref, wout_ref,
                   ng_ref, ba2_ref, glag_ref, cw_ref, cb_ref, dtb_ref, alog_ref, dsk_ref, ssdg_ref, fg_ref,
                   tril3_ref, km_ref, tril4_ref, vm_ref, stm_ref, bm_ref, tril8_ref, diag8_ref, xm_ref, hm_ref, e3_ref,
                   y_ref, st_out, hc_out, conv_out,
                   p_ref, hn_ref, o_buf, y_buf, st_ref, hc_ref):
    t = pl.program_id(1)
    c = CHUNK
    xbc_cols = slice(C_XBC, C_XBC + SSD_CONV_DIM)

    @pl.when(t == 0)
    def _():
        st_ref[...] = jnp.zeros_like(st_ref)
        hc_ref[...] = jnp.zeros_like(hc_ref)
        p_ref[0:SUBLANES, xbc_cols] = jnp.zeros((SUBLANES, SSD_CONV_DIM), F32)

    def norm_body(i, carry):
        rows = pl.ds(pl.multiple_of(i * c, c), c)
        xb = x_ref[rows, :]
        hn_ref[rows, :] = (xb * _rms_scale(xb) * ng_ref[...]).astype(BF16)
        return carry

    lax.fori_loop(0, T // c, norm_body, 0)

    _project(hn_ref, w_ref, p_ref, SUBLANES, T)

    a_pad = _a_pad(alog_ref)

    def chunk_body(ci, carry):
        r0 = pl.multiple_of(ci * c, c)
        rows = pl.ds(r0 + SUBLANES, c)
        misc = p_ref[rows, C_MISC:C_MISC + LANES]

        pre = _dot(misc.astype(BF16), wa2_ref[...]) + ba2_ref[...]
        log_a = (jnp.minimum(pre, 0.0) - jnp.log1p(jnp.exp(-jnp.abs(pre)))) * (1.0 / GLA_GATE_TEMP)
        b = _dot(tril3_ref[...], jnp.concatenate(_split3(log_a), axis=0))
        b_last = b[c - 1:c, :]
        q = p_ref[rows, C_Q:C_Q + GLA_QK]
        k = p_ref[rows, C_K:C_K + GLA_QK]
        v = p_ref[rows, C_V:C_V + GLA_V]
        qs = (q * (GLA_DK ** -0.5) * jnp.exp(b)).astype(BF16)
        ks = (k * jnp.exp(-b)).astype(BF16)
        kd = (k * jnp.exp(b_last - b)).astype(BF16)
        v_bf = v.astype(BF16)
        kt = jnp.concatenate([ks] * GLA_HEADS, axis=0) * km_ref[...]
        a_cat = _dot_nt(qs, kt) * tril4_ref[...]
        vbd = jnp.concatenate([v_bf] * GLA_HEADS, axis=0) * vm_ref[...]
        st = st_ref[...]
        o = _dot(a_cat.astype(BF16), vbd) + _dot_nt(qs, st.astype(BF16))
        st_ref[...] = st * jnp.exp(b_last) + _dot_tn(v_bf, kd) * stm_ref[...]
        g = p_ref[rows, C_G:C_G + GLA_V]
        o = _head_rmsnorm(o, GLA_DV) * glag_ref[...] * (g * _sigmoid(g))
        o_buf[pl.ds(r0, c), :] = o.astype(BF16)

        xe = p_ref[pl.ds(r0, c + SUBLANES), xbc_cols]
        acc = xe[SUBLANES:] * cw_ref[CONV_W - 1:CONV_W, :]
        for s in range(1, CONV_W):
            acc = acc + pltpu.roll(xe, s, 0)[SUBLANES:] * cw_ref[CONV_W - 1 - s:CONV_W - s, :]
        xc = acc + cb_ref[...]
        xc = xc * _sigmoid(xc)
        xs = xc[:, 0:SSD_INNER]
        bmat = xc[:, SSD_INNER:SSD_INNER + SSD_BC]
        c_bf = xc[:, SSD_INNER + SSD_BC:].astype(BF16)
        b_bf = bmat.astype(BF16)

        dtc = _softplus(misc + dtb_ref[...])
        cum_c = _dot(tril3_ref[...], jnp.concatenate(_split3(dtc * a_pad), axis=0))
        both = jnp.concatenate([jnp.concatenate(_split3(cum_c), axis=1),
                                jnp.concatenate(_split3(dtc), axis=1)], axis=0)
        ex = _dot(both, e3_ref[...])
        cum_e = ex[0:c]
        dt_e = ex[c:2 * c]
        cum_row = jnp.sum(cum_e * diag8_ref[...], axis=0, keepdims=True)
        decay = jnp.exp(jnp.minimum(cum_e - cum_row, 0.0)) * tril8_ref[...]
        bt = jnp.concatenate([b_bf] * SSD_HEADS, axis=0) * bm_ref[...]
        m_bf = (_dot_nt(c_bf, bt) * decay).astype(BF16)
        xdt = xs * dt_e
        xdt_bf = xdt.astype(BF16)
        ys = []
        for pr in range(SSD_HEADS // 2):
            cols = slice(pr * LANES, (pr + 1) * LANES)
            xbd = jnp.concatenate([xdt_bf[:, cols]] * 2, axis=0) * xm_ref[...]
            ys.append(_dot(m_bf[:, cols], xbd))
        hc = hc_ref[...]
        y = jnp.concatenate(ys, axis=-1) + _dot(c_bf, hc.astype(BF16)) * jnp.exp(cum_e) + dsk_ref[...] * xs
        c_last = cum_e[c - 1:c, :]
        xd = (xdt * jnp.exp(c_last - cum_e)).astype(BF16)
        hc_ref[...] = hc * jnp.exp(c_last) + _dot_tn(b_bf, xd) * hm_ref[...]
        z = p_ref[rows, C_Z:C_Z + SSD_INNER]
        y = _head_rmsnorm(y * (z * _sigmoid(z)), SSD_INNER // SSD_GROUPS) * ssdg_ref[...]
        y_buf[pl.ds(r0, c), :] = y.astype(BF16)
        return carry

    lax.fori_loop(0, T // c, chunk_body, 0)

    rb = 2 * c
    def out_body(i, carry):
        r0 = pl.multiple_of(i * rb, rb)
        rows = pl.ds(r0, rb)
        prow = pl.ds(r0 + SUBLANES, rb)
        y_ref[rows, :] = _merge_out(x_ref[rows, :], o_buf[rows, :], y_buf[rows, :],
                                    p_ref[prow, C_GA:C_GA + D_MODEL], p_ref[prow, C_GB:C_GB + D_MODEL],
                                    wbg_ref, wbs_ref, wout_ref, fg_ref)
        return carry

    lax.fori_loop(0, T // rb, out_body, 0)

    p_ref[0:SUBLANES, xbc_cols] = p_ref[T:T + SUBLANES, xbc_cols]

    @pl.when(t == pl.num_programs(1) - 1)
    def _():
        st_out[...] = st_ref[...]
        hc_out[...] = hc_ref[...]
        conv_out[...] = p_ref[T:T + SUBLANES, xbc_cols]


def _prompt_call(x, params, consts):
    B, L, _ = x.shape
    T = min(PROMPT_T, L)
    assert L % T == 0 and T % (2 * CHUNK) == 0
    full = lambda a: pl.BlockSpec(a.shape, lambda b, t: (0,) * a.ndim)
    ins = [params[n] for n in _PARAM_ORDER] + [consts[n] for n in _CONST_ORDER]
    out_shape = (
        jax.ShapeDtypeStruct((B, L, D_MODEL), F32),
        jax.ShapeDtypeStruct((B, GLA_V, GLA_QK), F32),
        jax.ShapeDtypeStruct((B, SSD_BC, SSD_INNER), F32),
        jax.ShapeDtypeStruct((B, SUBLANES, SSD_CONV_DIM), F32),
    )
    tok = pl.BlockSpec((None, T, D_MODEL), lambda b, t: (b, t, 0))
    per_b = lambda s: pl.BlockSpec((None,) + s, lambda b, t: (b, 0, 0))
    return pl.pallas_call(
        functools.partial(_prompt_kernel, T),
        grid=(B, L // T),
        in_specs=[tok] + [full(a) for a in ins],
        out_specs=(tok, per_b((GLA_V, GLA_QK)), per_b((SSD_BC, SSD_INNER)), per_b((SUBLANES, SSD_CONV_DIM))),
        out_shape=out_shape,
        scratch_shapes=[
            pltpu.VMEM((T + SUBLANES, N_PROJ), F32),
            pltpu.VMEM((T, D_MODEL), BF16),
            pltpu.VMEM((T, GLA_V), BF16),
            pltpu.VMEM((T, SSD_INNER), BF16),
            pltpu.VMEM((GLA_V, GLA_QK), F32),
            pltpu.VMEM((SSD_BC, SSD_INNER), F32),
        ],
        compiler_params=pltpu.CompilerParams(dimension_semantics=("arbitrary", "arbitrary"),
                                             vmem_limit_bytes=VMEM_LIMIT),
        name="prompt_step",
    )(x, *ins)


def _unpack_prompt_states(st, hc, conv):
    B = st.shape[0]
    st5 = st.reshape(B, GLA_HEADS, GLA_DV, GLA_HEADS, GLA_DK)
    gla = jnp.stack([st5[:, h, :, h, :] for h in range(GLA_HEADS)], axis=1).swapaxes(-1, -2)
    hc5 = hc.reshape(B, SSD_GROUPS, SSD_STATE, SSD_HEADS, SSD_HEADDIM)
    rep = SSD_HEADS // SSD_GROUPS
    ssm = jnp.stack([hc5[:, h // rep, :, h, :] for h in range(SSD_HEADS)], axis=1).swapaxes(-1, -2)
    return gla[None], ssm[None], conv[None, :, SUBLANES - (CONV_W - 1):, :]


def _sample_kernel(NB, x_ref, sg_ref, ss_ref, sc_ref, w_ref, wa2_ref, wbg_ref, wbs_ref, wout_ref,
                   ng_ref, ba2_ref, glag_ref, cw_ref, cb_ref, dtb_ref, alog_ref, dsk_ref, ssdg_ref, fg_ref, e3_ref,
                   y_ref, sg_out, ss_out, sc_out,
                   p_ref, hn_ref, qt_ref, kt_ref, ebt_ref, xdtt_ref, elat_ref, v_ref, xs_ref, b_ref, c_ref,
                   o_ref, yr_ref):
    i = pl.program_id(0)
    sb = SAMPLE_SB
    cd = SSD_CONV_DIM

    @pl.when(i == 0)
    def _():
        xb = x_ref[...]
        hn_ref[...] = (xb * _rms_scale(xb) * ng_ref[...]).astype(BF16)
        _project(hn_ref, w_ref, p_ref, 0, NB)
        misc = p_ref[:, C_MISC:C_MISC + LANES]
        pre = _dot(misc.astype(BF16), wa2_ref[...]) + ba2_ref[...]
        log_a = (jnp.minimum(pre, 0.0) - jnp.log1p(jnp.exp(-jnp.abs(pre)))) * (1.0 / GLA_GATE_TEMP)
        qt_ref[...] = (p_ref[:, C_Q:C_Q + GLA_QK] * (GLA_DK ** -0.5)).T.astype(BF16)
        kt_ref[...] = p_ref[:, C_K:C_K + GLA_QK].T.astype(BF16)
        ebt_ref[...] = jnp.concatenate(_split3(jnp.exp(log_a).T), axis=1)
        v_ref[...] = p_ref[:, C_V:C_V + GLA_V]

        xbc = p_ref[:, C_XBC:C_XBC + cd]
        xc = xbc * cw_ref[CONV_W - 1:CONV_W, :] + cb_ref[...]
        for s in range(CONV_W - 1):
            xc = xc + sc_ref[:, s * cd:(s + 1) * cd] * cw_ref[s:s + 1, :]
        sc_out[:, 0:(CONV_W - 2) * cd] = sc_ref[:, cd:(CONV_W - 1) * cd]
        sc_out[:, (CONV_W - 2) * cd:(CONV_W - 1) * cd] = xbc
        xc = xc * _sigmoid(xc)
        xs = xc[:, 0:SSD_INNER]
        xs_ref[...] = xs
        b_ref[...] = xc[:, SSD_INNER:SSD_INNER + SSD_BC]
        c_ref[...] = xc[:, SSD_INNER + SSD_BC:]
        dtc = _softplus(misc + dtb_ref[...])
        both = jnp.concatenate([jnp.concatenate(_split3(dtc * _a_pad(alog_ref)), axis=1),
                                jnp.concatenate(_split3(dtc), axis=1)], axis=0)
        ex = _dot(both, e3_ref[...])
        xdtt_ref[...] = (xs * ex[NB:2 * NB]).T.astype(BF16)
        elat_ref[...] = jnp.concatenate(_split3(jnp.exp(ex[0:NB]).T), axis=1)

    blk = pl.ds(pl.multiple_of(i * sb, sb), sb)
    vblk = v_ref[blk, :]
    bblk = b_ref[blk, :]
    cblk = c_ref[blk, :].astype(BF16)
    row_iota = lax.broadcasted_iota(jnp.int32, (NB, LANES), 0)
    sub_iota = lax.broadcasted_iota(jnp.int32, (sb, 1), 0)
    rep = SSD_HEADS // SSD_GROUPS
    o_blk = jnp.zeros((sb, GLA_V), F32)
    y_blk = jnp.zeros((sb, SSD_INNER), F32)
    for s in range(sb):
        sel = (row_iota == i * sb + s).astype(BF16)
        sel3 = jnp.concatenate([sel] * 3, axis=0)
        qcol = _dot(qt_ref[...], sel)
        kcol = _dot(kt_ref[...], sel)
        ecol = _dot(ebt_ref[...], sel3)
        o_parts = []
        for h in range(GLA_HEADS):
            hd = slice(h * GLA_DK, (h + 1) * GLA_DK)
            s_new = ecol[hd] * sg_ref[s, h] + kcol[hd] * vblk[s:s + 1, h * GLA_DV:(h + 1) * GLA_DV]
            sg_out[s, h] = s_new
            o_parts.append(jnp.sum(qcol[hd] * s_new, axis=0, keepdims=True))
        o_blk = jnp.where(sub_iota == s, jnp.concatenate(o_parts, axis=-1), o_blk)

        xcol = _dot(xdtt_ref[...], sel)[:, 0:SSD_STATE]
        lcol = _dot(elat_ref[...], sel3)[:, 0:SSD_STATE]
        y_parts = []
        for g in range(SSD_GROUPS):
            gn = slice(g * SSD_STATE, (g + 1) * SSD_STATE)
            h_new = []
            for h in range(g * rep, (g + 1) * rep):
                hp = slice(h * SSD_HEADDIM, (h + 1) * SSD_HEADDIM)
                hn_ = lcol[hp] * ss_ref[s, h] + xcol[hp] * bblk[s:s + 1, gn]
                ss_out[s, h] = hn_
                h_new.append(hn_.astype(BF16))
            y_parts.append(_dot_nt(cblk[:, gn], jnp.concatenate(h_new, axis=0)))
        y_blk = jnp.where(sub_iota == s, jnp.concatenate(y_parts, axis=-1), y_blk)
    o_ref[blk, :] = o_blk
    yr_ref[blk, :] = y_blk

    @pl.when(i == pl.num_programs(0) - 1)
    def _():
        g = p_ref[:, C_G:C_G + GLA_V]
        o = _head_rmsnorm(o_ref[...], GLA_DV) * glag_ref[...] * (g * _sigmoid(g))
        z = p_ref[:, C_Z:C_Z + SSD_INNER]
        y = yr_ref[...] + dsk_ref[...] * xs_ref[...]
        y = _head_rmsnorm(y * (z * _sigmoid(z)), SSD_INNER // SSD_GROUPS) * ssdg_ref[...]
        y_ref[...] = _merge_out(x_ref[...], o.astype(BF16), y.astype(BF16),
                                p_ref[:, C_GA:C_GA + D_MODEL], p_ref[:, C_GB:C_GB + D_MODEL],
                                wbg_ref, wbs_ref, wout_ref, fg_ref)


def _sample_call(x, state_gla, state_ssm, state_conv, params, consts):
    NB = x.shape[0]
    sb = SAMPLE_SB
    assert NB % LANES == 0 and NB % sb == 0
    x2 = x.reshape(NB, D_MODEL)
    sc2 = state_conv.reshape(NB, (CONV_W - 1) * SSD_CONV_DIM)
    ins = [params[n] for n in _PARAM_ORDER] + [consts["e3"]]
    full = lambda a: pl.BlockSpec(a.shape, lambda i: (0,) * a.ndim)
    sg_spec = pl.BlockSpec((sb, GLA_HEADS, GLA_DK, GLA_DV), lambda i: (i, 0, 0, 0))
    ss_spec = pl.BlockSpec((sb, SSD_HEADS, SSD_HEADDIM, SSD_STATE), lambda i: (i, 0, 0, 0))
    out_shape = (
        jax.ShapeDtypeStruct((NB, D_MODEL), F32),
        jax.ShapeDtypeStruct(state_gla.shape, F32),
        jax.ShapeDtypeStruct(state_ssm.shape, F32),
        jax.ShapeDtypeStruct(sc2.shape, F32),
    )
    y, sg, ss, sc = pl.pallas_call(
        functools.partial(_sample_kernel, NB),
        grid=(NB // sb,),
        in_specs=[full(x2), sg_spec, ss_spec, full(sc2)] + [full(a) for a in ins],
        out_specs=(full(x2), sg_spec, ss_spec, full(sc2)),
        out_shape=out_shape,
        scratch_shapes=[
            pltpu.VMEM((NB, N_PROJ), F32),
            pltpu.VMEM((NB, D_MODEL), BF16),
            pltpu.VMEM((GLA_QK, NB), BF16),
            pltpu.VMEM((GLA_QK, NB), BF16),
            pltpu.VMEM((GLA_QK, 3 * NB), BF16),
            pltpu.VMEM((SSD_INNER, NB), BF16),
            pltpu.VMEM((SSD_INNER, 3 * NB), BF16),
            pltpu.VMEM((NB, GLA_V), F32),
            pltpu.VMEM((NB, SSD_INNER), F32),
            pltpu.VMEM((NB, SSD_BC), F32),
            pltpu.VMEM((NB, SSD_BC), F32),
            pltpu.VMEM((NB, GLA_V), F32),
            pltpu.VMEM((NB, SSD_INNER), F32),
        ],
        compiler_params=pltpu.CompilerParams(dimension_semantics=("arbitrary",), vmem_limit_bytes=VMEM_LIMIT),
        name="decode_step",
    )(x2, state_gla, state_ssm, sc2, *ins)
    return y.reshape(x.shape), sg, ss, sc.reshape(state_conv.shape)


def kernel(x_prompt, x_sample, state_gla, state_ssm, state_conv, norm_g, w_in, w_a2, b_a2, gla_norm_g, conv_w, conv_b,
           dt_bias, a_log, d_skip, ssd_norm_g, w_br_gla, w_br_ssd, w_out, final_norm_g):
    params = _pack_params(norm_g[0], w_in[0], w_a2[0], b_a2[0], gla_norm_g[0], conv_w[0], conv_b[0], dt_bias[0],
                          a_log[0], d_skip[0], ssd_norm_g[0], w_br_gla[0], w_br_ssd[0], w_out[0], final_norm_g)
    consts = _consts()
    y_p, st, hc, conv = _prompt_call(x_prompt, params, consts)
    gla_p, ssm_p, conv_p = _unpack_prompt_states(st, hc, conv)
    y_s, gla_s, ssm_s, conv_s = _sample_call(x_sample, state_gla[0], state_ssm[0], state_conv[0], params, consts)
    return (y_p, y_s, gla_p, ssm_p, conv_p, gla_s[None], ssm_s[None], conv_s[None])
```

```python
import functools

import numpy as np
import jax
import jax.numpy as jnp
from jax import lax
from jax.experimental import pallas as pl
from jax.experimental.pallas import tpu as pltpu

F32 = jnp.float32
BF16 = jnp.bfloat16

D_MODEL = 1024
GLA_HEADS = 4
GLA_DK = 64
GLA_DV = 128
GLA_QK = GLA_HEADS * GLA_DK
GLA_V = GLA_HEADS * GLA_DV
GLA_RANK = 16
GLA_GATE_TEMP = 16.0
SSD_HEADS = 8
SSD_HEADDIM = 64
SSD_INNER = SSD_HEADS * SSD_HEADDIM
SSD_GROUPS = 2
SSD_STATE = 64
SSD_BC = SSD_GROUPS * SSD_STATE
CONV_W = 4
SSD_CONV_DIM = SSD_INNER + 2 * SSD_BC
CHUNK = 64
EPS = 1e-6
PROJ_SIZES = (GLA_QK, GLA_QK, GLA_V, GLA_V, GLA_RANK, SSD_INNER, SSD_CONV_DIM, SSD_HEADS, D_MODEL, D_MODEL)

LANES = 128
SUBLANES = 8

NG = 4
GW = 1280
PW = 256
GA_SPLIT = GW - SSD_INNER
SEG = dict(q=(0, 0), k=(0, GLA_QK), v=(0, 2 * GLA_QK), misc=(0, 2 * GLA_QK + GLA_V),
           g=(1, 0), xbc=(1, GLA_V),
           z=(2, 0), ga0=(2, SSD_INNER),
           ga1=(3, 0), gb=(3, D_MODEL - GA_SPLIT))
MISC_DT = GLA_RANK

_OFF = dict(zip(("q", "k", "v", "g", "a_lr", "z", "xbc", "dt", "ga", "gb"),
                (int(o) for o in np.cumsum((0,) + PROJ_SIZES[:-1]))))
_SRC = dict(q=("main", _OFF["q"], GLA_QK), k=("main", _OFF["k"], GLA_QK), v=("main", _OFF["v"], GLA_V),
            misc=("misc", 0, LANES), g=("main", _OFF["g"], GLA_V), xbc=("main", _OFF["xbc"], SSD_CONV_DIM),
            z=("main", _OFF["z"], SSD_INNER), ga0=("gates", 0, GA_SPLIT), ga1=("gates", GA_SPLIT, D_MODEL - GA_SPLIT),
            gb=("gates", D_MODEL, D_MODEL))
_PIECES = [(SEG[n][0], SEG[n][1] + o, min(PW, size - o), src, row + o)
           for n, (src, row, size) in _SRC.items() for o in range(0, size, PW)]

XH = 2
PROMPT_T = 256
SAMPLE_SB = 16
SSD_PB = 64
VMEM_LIMIT = 56 * 1024 * 1024


def _split3(x):
    hi = x.astype(BF16)
    r1 = x - hi.astype(F32)
    mid = r1.astype(BF16)
    lo = (r1 - mid.astype(F32)).astype(BF16)
    return hi, mid, lo


def _sigmoid(x):
    return 1.0 / (1.0 + jnp.exp(-x))


def _softplus(x):
    return jnp.maximum(x, 0.0) + jnp.log1p(jnp.exp(-jnp.abs(x)))


def _rms_scale(x):
    return lax.rsqrt(jnp.mean(x * x, axis=-1, keepdims=True) + EPS)


def _dot(a, b):
    return jnp.dot(a, b, preferred_element_type=F32)


def _dot_nt(a, b):
    return lax.dot_general(a, b, (((1,), (1,)), ((), ())), preferred_element_type=F32)


def _dot_tn(a, b):
    return lax.dot_general(a, b, (((0,), (0,)), ((), ())), preferred_element_type=F32)


def _consts():
    r = np.arange

    def m(a, dt):
        return jnp.asarray(a.astype(np.float32), dtype=dt)

    c = CHUNK
    return dict(
        tril3=m(r(c)[:, None] >= (r(3 * c)[None, :] % c), BF16),
        km=m(r(GLA_HEADS * c)[:, None] // c == r(GLA_QK)[None, :] // GLA_DK, BF16),
        tril4=m(r(c)[:, None] >= r(GLA_HEADS * c)[None, :] % c, F32),
        vm=m(r(GLA_HEADS * c)[:, None] // c == r(GLA_V)[None, :] // GLA_DV, BF16),
        stm=m(r(GLA_V)[:, None] // GLA_DV == r(GLA_QK)[None, :] // GLA_DK, F32),
        bm=m(r(SSD_HEADS * c)[:, None] // (c * SSD_HEADS // SSD_GROUPS) == r(SSD_BC)[None, :] // SSD_STATE, BF16),
        tril8=m(r(c)[:, None] >= r(SSD_HEADS * c)[None, :] % c, F32),
        diag8=m(r(c)[:, None] == r(SSD_HEADS * c)[None, :] % c, F32),
        xm=m(r(XH * c)[:, None] // c == r(XH * SSD_HEADDIM)[None, :] // SSD_HEADDIM, BF16),
        hm=m(r(SSD_BC)[:, None] // SSD_STATE == r(SSD_INNER)[None, :] // (SSD_INNER // SSD_GROUPS), F32),
        e2=m((r(2 * LANES)[:, None] % LANES) == MISC_DT + r(SSD_INNER)[None, :] // SSD_HEADDIM, BF16),
    )


_CONST_ORDER = ("tril3", "km", "tril4", "vm", "stm", "bm", "tril8", "diag8", "xm", "hm", "e2")


def _pack_params(norm_g, w_in, w_a2, b_a2, gla_norm_g, conv_w, conv_b, dt_bias, a_log, d_skip, ssd_norm_g,
                 w_br_gla, w_br_ssd, w_out, final_norm_g):
    wt = w_in.T
    wm = wt[:_OFF["dt"]].astype(BF16)
    wg = wt[_OFF["ga"]:].astype(BF16)
    wx = jnp.concatenate([wt[_OFF["a_lr"]:_OFF["z"]], wt[_OFF["dt"]:_OFF["ga"]],
                          jnp.zeros((LANES - GLA_RANK - SSD_HEADS, D_MODEL), wt.dtype)], axis=0).astype(BF16)
    wa2 = jnp.concatenate([w_a2, jnp.zeros((LANES - GLA_RANK, GLA_QK), w_a2.dtype)], axis=0).astype(BF16)

    def misc_row(p):
        return jnp.zeros((1, LANES), F32).at[0, MISC_DT:MISC_DT + SSD_HEADS].set(p.astype(F32))

    return dict(
        wm=wm, wg=wg, wx=wx, wa2=wa2,
        wbg=w_br_gla.astype(BF16), wbs=w_br_ssd.astype(BF16), wout=w_out.astype(BF16),
        ng=norm_g.reshape(1, D_MODEL), ba2=b_a2.reshape(1, GLA_QK), glag=gla_norm_g.reshape(1, GLA_V),
        cw=conv_w, cb=conv_b.reshape(1, SSD_CONV_DIM), dtb=misc_row(dt_bias), alog=misc_row(a_log),
        dsk=jnp.repeat(d_skip.astype(F32), SSD_HEADDIM).reshape(1, SSD_INNER),
        ssdg=ssd_norm_g.reshape(1, SSD_INNER), fg=final_norm_g.reshape(1, D_MODEL),
    )


_PARAM_ORDER = ("wm", "wg", "wx", "wa2", "wbg", "wbs", "wout",
                "ng", "ba2", "glag", "cw", "cb", "dtb", "alog", "dsk", "ssdg", "fg")


def _proj_pieces(hn, w_refs, store):
    def piece(gi, l0, width, src, row):
        def emit():
            store(gi, l0, width, _dot_nt(hn(), w_refs[src][row:row + width, :]))
        return emit
    return [piece(*p) for p in _PIECES]


def _a_pad(alog_ref):
    lane = lax.broadcasted_iota(jnp.int32, (1, LANES), 1)
    is_dt = (lane >= MISC_DT) & (lane < MISC_DT + SSD_HEADS)
    return jnp.where(is_dt, -jnp.exp(alog_ref[...]), 0.0)


def _log_sigmoid(x):
    return jnp.minimum(x, 0.0) - jnp.log(1.0 + jnp.exp(-jnp.abs(x)))


def _merge_out(x, o_bf, y_bf, gate_a, gate_b, wbg_ref, wbs_ref, wout_ref, fg_ref):
    merged = _sigmoid(gate_a) * _dot(o_bf, wbg_ref[...]) + _sigmoid(gate_b) * _dot(y_bf, wbs_ref[...])
    out = x + _dot(merged.astype(BF16), wout_ref[...])
    return out * _rms_scale(out) * fg_ref[...]


def _head_rmsnorm(x, width):
    parts = []
    for c0 in range(0, x.shape[-1], width):
        blk = x[:, c0:c0 + width]
        parts.append(blk * _rms_scale(blk))
    return jnp.concatenate(parts, axis=-1)


def _prompt_kernel(T, x_ref, xn_ref, wm_ref, wg_ref, wx_ref, wa2_ref, wbg_ref, wbs_ref, wout_ref,
                   ng_ref, ba2_ref, glag_ref, cw_ref, cb_ref, dtb_ref, alog_ref, dsk_ref, ssdg_ref, fg_ref,
                   tril3_ref, km_ref, tril4_ref, vm_ref, stm_ref, bm_ref, tril8_ref, diag8_ref, xm_ref, hm_ref, e2_ref,
                   y_ref, st_out, hc_out, conv_out,
                   pa_ref, pb_ref, hn_ref, o_buf, y_buf, mg_buf, st_ref, hc_ref):
    t = pl.program_id(1)
    step = pl.program_id(0) * pl.num_programs(1) + t
    c = CHUNK
    tok = slice(SUBLANES, SUBLANES + T)
    last_rows = slice(T, T + SUBLANES)
    gx, lx = SEG["xbc"]
    xbc_l = slice(lx, lx + SSD_CONV_DIM)

    def normalise(src_ref, row0):
        for r0 in range(0, T, c):
            xb = src_ref[row0 + r0:row0 + r0 + c, :]
            hn_ref[r0:r0 + c, :] = (xb * _rms_scale(xb) * ng_ref[...]).astype(BF16)

    w_refs = dict(main=wm_ref, gates=wg_ref, misc=wx_ref)

    def projection(dst_ref):
        def store(gi, l0, width, value):
            dst_ref[gi, tok, l0:l0 + width] = value
        return _proj_pieces(lambda: hn_ref[...], w_refs, store)

    @pl.when(step == 0)
    def _():
        normalise(x_ref, 0)
        for emit in projection(pa_ref):
            emit()

    @pl.when(t == 0)
    def _():
        st_ref[...] = jnp.zeros_like(st_ref)
        hc_ref[...] = jnp.zeros_like(hc_ref)
        pa_ref[gx, 0:SUBLANES, xbc_l] = jnp.zeros((SUBLANES, SSD_CONV_DIM), F32)

    a_pad = _a_pad(alog_ref)

    def run_block(pc_ref, pn_ref, ob, deferred):
      def seg(name, rows, width):
        gi, l0 = SEG[name]
        return pc_ref[gi, rows, l0:l0 + width]


      def stage1(ci):
        r0 = ci * c
        rows = slice(r0 + SUBLANES, r0 + SUBLANES + c)
        misc = seg("misc", rows, LANES)
        pre = _dot(misc.astype(BF16), wa2_ref[...]) + ba2_ref[...]
        dtc = _softplus(misc + dtb_ref[...])
        cum_c = _dot(tril3_ref[...], jnp.concatenate(_split3(dtc * a_pad), axis=0))
        return dict(r0=r0, rows=rows, pre=pre, dtc=dtc, cum_c=cum_c)

      def stage2(d):
        log_a = _log_sigmoid(d["pre"]) * (1.0 / GLA_GATE_TEMP)
        d["b"] = _dot(tril3_ref[...], jnp.concatenate(_split3(log_a), axis=0))
        both = jnp.concatenate([jnp.concatenate(_split3(d["cum_c"])[:2], axis=1),
                                jnp.concatenate(_split3(d["dtc"])[:2], axis=1)], axis=0)
        d["ex"] = _dot(both, e2_ref[...])

      def stage3(d):
        rows, b = d["rows"], d["b"]
        b_last = b[c - 1:c, :]
        k = seg("k", rows, GLA_QK)
        qs = (seg("q", rows, GLA_QK) * (GLA_DK ** -0.5) * jnp.exp(b)).astype(BF16)
        ks = (k * jnp.exp(-b)).astype(BF16)
        kt = jnp.concatenate([ks] * GLA_HEADS, axis=0) * km_ref[...]
        d["a_cat"] = (_dot_nt(qs, kt) * tril4_ref[...]).astype(BF16)
        d["qs"] = qs
        d["kd"] = (k * jnp.exp(b_last - b)).astype(BF16)
        d["eb_last"] = jnp.exp(b_last)

        xe = pc_ref[gx, d["r0"]:d["r0"] + c + SUBLANES, xbc_l]
        x1 = pltpu.roll(xe, 1, 0)
        near = xe[SUBLANES:] * cw_ref[3:4, :] + x1[SUBLANES:] * cw_ref[2:3, :]
        far = pltpu.roll(xe * cw_ref[1:2, :] + x1 * cw_ref[0:1, :], 2, 0)[SUBLANES:]
        xc = near + far + cb_ref[...]
        xc = xc * _sigmoid(xc)
        xs = xc[:, 0:SSD_INNER]
        b_bf = xc[:, SSD_INNER:SSD_INNER + SSD_BC].astype(BF16)
        c_bf = xc[:, SSD_INNER + SSD_BC:].astype(BF16)
        bt = jnp.concatenate([b_bf] * SSD_HEADS, axis=0) * bm_ref[...]
        cum_e = d["ex"][0:c]
        cum_row = jnp.sum(cum_e * diag8_ref[...], axis=0, keepdims=True)
        decay = jnp.exp(jnp.minimum(cum_e - cum_row, 0.0)) * tril8_ref[...]
        d["m_bf"] = (_dot_nt(c_bf, bt) * decay).astype(BF16)
        xdt = xs * d["ex"][c:2 * c]
        c_last = cum_e[c - 1:c, :]
        d["xd"] = (xdt * jnp.exp(c_last - cum_e)).astype(BF16)
        d["xdt_bf"] = xdt.astype(BF16)
        d["ec_last"] = jnp.exp(c_last)
        d["ecum"] = jnp.exp(cum_e)
        d["xs"], d["b_bf"], d["c_bf"] = xs, b_bf, c_bf

      def stage4(d):
        v_bf = seg("v", d["rows"], GLA_V).astype(BF16)
        vbd = jnp.concatenate([v_bf] * GLA_HEADS, axis=0) * vm_ref[...]
        st = st_ref[...]
        d["o"] = _dot(d["a_cat"], vbd) + _dot_nt(d["qs"], st.astype(BF16))
        st_ref[...] = st * d["eb_last"] + _dot_tn(v_bf, d["kd"]) * stm_ref[...]
        ys = []
        for quad in range(SSD_HEADS // XH):
            cols = slice(quad * XH * SSD_HEADDIM, (quad + 1) * XH * SSD_HEADDIM)
            xbd = jnp.concatenate([d["xdt_bf"][:, cols]] * XH, axis=0) * xm_ref[...]
            ys.append(_dot(d["m_bf"][:, cols], xbd))
        hc = hc_ref[...]
        d["y"] = (jnp.concatenate(ys, axis=-1) + _dot(d["c_bf"], hc.astype(BF16)) * d["ecum"]
                  + dsk_ref[...] * d["xs"])
        hc_ref[...] = hc * d["ec_last"] + _dot_tn(d["b_bf"], d["xd"]) * hm_ref[...]

      def stage5(d):
        g = seg("g", d["rows"], GLA_V)
        o = _head_rmsnorm(d["o"], GLA_DV) * glag_ref[...] * (g * _sigmoid(g))
        o_buf[ob, pl.ds(d["r0"], c), :] = o.astype(BF16)
        z = seg("z", d["rows"], SSD_INNER)
        y = _head_rmsnorm(d["y"] * (z * _sigmoid(z)), SSD_INNER // SSD_GROUPS) * ssdg_ref[...]
        y_buf[ob, pl.ds(d["r0"], c), :] = y.astype(BF16)

      fill = projection(pn_ref) + deferred
      nch = T // c
      n_slots = 3 + 2 * nch
      it = iter(fill)

      counts = [0, 1, 1, 2, 2, 2, 2, 2, 3, 3, 2]
      assert sum(counts) == len(fill) and len(counts) == n_slots

      def gap(slot):
        for _ in range(counts[slot]):
            next(it)()

      gap(0)
      ds = [stage1(ci) for ci in range(nch)]
      gap(1)
      for d in ds:
        stage2(d)
      gap(2)
      for k, d in enumerate(ds):
        stage3(d)
        gap(3 + k)
      for k, d in enumerate(ds):
        stage4(d)
        gap(3 + nch + k)
        stage5(d)
      pn_ref[gx, 0:SUBLANES, xbc_l] = pc_ref[gx, last_rows, xbc_l]

    def out_pieces(p_ref, row0, ob):
        prow = slice(SUBLANES, SUBLANES + T)
        xrows = slice(row0, row0 + T)
        nq = D_MODEL // PW

        def gate_cols(name_lo, name_hi, split, j):
            c0 = j * PW
            name, off = (name_lo, c0) if c0 < split else (name_hi, c0 - split)
            gi, l0 = SEG[name]
            return p_ref[gi, prow, l0 + off:l0 + off + PW]

        def merge(j):
            def emit():
                cols = slice(j * PW, (j + 1) * PW)
                ga = gate_cols("ga0", "ga1", GA_SPLIT, j)
                gb = gate_cols("gb", "gb", D_MODEL, j)
                m = (_sigmoid(ga) * _dot(o_buf[ob], wbg_ref[:, cols])
                     + _sigmoid(gb) * _dot(y_buf[ob], wbs_ref[:, cols]))
                mg_buf[:, cols] = m.astype(BF16)
            return emit

        def project(j):
            def emit():
                cols = slice(j * PW, (j + 1) * PW)
                y_ref[xrows, cols] = x_ref[xrows, cols] + _dot(mg_buf[...], wout_ref[:, cols])
            return emit

        def final_norm():
            for r0 in range(row0, row0 + T, c):
                v = y_ref[r0:r0 + c, :]
                y_ref[r0:r0 + c, :] = v * _rms_scale(v) * fg_ref[...]

        return [merge(j) for j in range(nq)] + [project(j) for j in range(nq)] + [final_norm]

    normalise(x_ref, T)
    run_block(pa_ref, pb_ref, 0, [])
    for emit in out_pieces(pa_ref, 0, 0):
        emit()
    normalise(xn_ref, 0)
    run_block(pb_ref, pa_ref, 1, [])
    for emit in out_pieces(pb_ref, T, 1):
        emit()

    @pl.when(t == pl.num_programs(1) - 1)
    def _():
        for h in range(GLA_HEADS):
            blk = st_ref[h * GLA_DV:(h + 1) * GLA_DV, :].T
            st_out[h] = blk[h * GLA_DK:(h + 1) * GLA_DK, :]
        hct = hc_ref[...].T
        rep = SSD_HEADS // SSD_GROUPS
        for h in range(SSD_HEADS):
            blk = hct[h * SSD_HEADDIM:(h + 1) * SSD_HEADDIM, :]
            if h // rep:
                blk = pltpu.roll(blk, SSD_BC - (h // rep) * SSD_STATE, 1)
            hc_out[h] = blk[:, 0:SSD_STATE]
        conv_out[...] = pb_ref[gx, last_rows, xbc_l]


def _prompt_call(x, params, consts):
    B, L, _ = x.shape
    T = PROMPT_T
    assert L % (2 * T) == 0 and T == NG * CHUNK and CONV_W == 4 and SSD_GROUPS == 2
    nt = L // (2 * T)
    full = lambda a: pl.BlockSpec(a.shape, lambda b, t: (0,) * a.ndim)
    ins = [params[n] for n in _PARAM_ORDER] + [consts[n] for n in _CONST_ORDER]
    out_shape = (
        jax.ShapeDtypeStruct((B, L, D_MODEL), F32),
        jax.ShapeDtypeStruct((B, GLA_HEADS, GLA_DK, GLA_DV), F32),
        jax.ShapeDtypeStruct((B, SSD_HEADS, SSD_HEADDIM, SSD_STATE), F32),
        jax.ShapeDtypeStruct((B, SUBLANES, SSD_CONV_DIM), F32),
    )

    def next_block(b, t):
        lin = jnp.minimum(b * nt + t + 1, B * nt - 1)
        return (lin // nt, 2 * (lin % nt), 0)

    tok = pl.BlockSpec((None, 2 * T, D_MODEL), lambda b, t: (b, t, 0))
    tok_next = pl.BlockSpec((None, T, D_MODEL), next_block)
    per_b = lambda s: pl.BlockSpec((None,) + s, lambda b, t: (b,) + (0,) * len(s))
    return pl.pallas_call(
        functools.partial(_prompt_kernel, T),
        grid=(B, nt),
        in_specs=[tok, tok_next] + [full(a) for a in ins],
        out_specs=(tok, per_b((GLA_HEADS, GLA_DK, GLA_DV)), per_b((SSD_HEADS, SSD_HEADDIM, SSD_STATE)),
                   per_b((SUBLANES, SSD_CONV_DIM))),
        out_shape=out_shape,
        scratch_shapes=[
            pltpu.VMEM((NG, T + SUBLANES, GW), F32),
            pltpu.VMEM((NG, T + SUBLANES, GW), F32),
            pltpu.VMEM((T, D_MODEL), BF16),
            pltpu.VMEM((2, T, GLA_V), BF16),
            pltpu.VMEM((2, T, SSD_INNER), BF16),
            pltpu.VMEM((T, D_MODEL), BF16),
            pltpu.VMEM((GLA_V, GLA_QK), F32),
            pltpu.VMEM((SSD_BC, SSD_INNER), F32),
        ],
        compiler_params=pltpu.CompilerParams(dimension_semantics=("arbitrary", "arbitrary"),
                                             vmem_limit_bytes=VMEM_LIMIT),
        name="prompt_step",
    )(x, x, *ins)


def _sample_kernel(NB, x_ref, sg_ref, ss_ref, sc_ref, wm_ref, wg_ref, wx_ref, wa2_ref, wbg_ref, wbs_ref, wout_ref,
                   ng_ref, ba2_ref, glag_ref, cw_ref, cb_ref, dtb_ref, alog_ref, dsk_ref, ssdg_ref, fg_ref,
                   y_ref, sg_out, ss_out, sc_out,
                   p_ref, hn_ref, qt_ref, kt_ref, ebt_ref, xdtt_ref, elat_ref, v_ref, xs_ref, bt_ref, ct_ref,
                   o_ref, yt_ref):
    i = pl.program_id(0)
    sb = SAMPLE_SB
    cd = SSD_CONV_DIM
    rep = SSD_HEADS // SSD_GROUPS

    def seg(name, width):
        gi, l0 = SEG[name]
        return p_ref[gi, :, l0:l0 + width]

    @pl.when(i == 0)
    def _():
        xb = x_ref[:, 0, :]
        hn_ref[...] = (xb * _rms_scale(xb) * ng_ref[...]).astype(BF16)
        def store(gi, l0, width, value):
            p_ref[gi, :, l0:l0 + width] = value
        for emit in _proj_pieces(lambda: hn_ref[...], dict(main=wm_ref, gates=wg_ref, misc=wx_ref), store):
            emit()
        misc = seg("misc", LANES)
        pre = _dot(misc.astype(BF16), wa2_ref[...]) + ba2_ref[...]
        log_a = _log_sigmoid(pre) * (1.0 / GLA_GATE_TEMP)
        qt_ref[...] = (seg("q", GLA_QK) * (GLA_DK ** -0.5)).T.astype(BF16)
        kt_ref[...] = seg("k", GLA_QK).T.astype(BF16)
        ebt_ref[...] = jnp.concatenate(_split3(jnp.exp(log_a).T), axis=1)
        v_ref[...] = seg("v", GLA_V)

        xbc = seg("xbc", cd)
        xc = xbc * cw_ref[CONV_W - 1:CONV_W, :] + cb_ref[...]
        for s in range(CONV_W - 1):
            xc = xc + sc_ref[s] * cw_ref[s:s + 1, :]
        for s in range(CONV_W - 2):
            sc_out[s] = sc_ref[s + 1]
        sc_out[CONV_W - 2] = xbc
        xc = xc * _sigmoid(xc)
        xs = xc[:, 0:SSD_INNER]
        xs_ref[...] = xs
        bt_ref[...] = xc[:, SSD_INNER:SSD_INNER + SSD_BC].T
        ct_ref[...] = xc[:, SSD_INNER + SSD_BC:].T
        dtc = _softplus(misc + dtb_ref[...])
        dtt = dtc.T
        elat_ref[...] = jnp.exp(dtc * _a_pad(alog_ref)).T
        xst = xs.T
        for h in range(SSD_HEADS):
            hp = slice(h * SSD_HEADDIM, (h + 1) * SSD_HEADDIM)
            xdtt_ref[hp, :] = xst[hp, :] * dtt[MISC_DT + h:MISC_DT + h + 1, :]

    head = i // (SSD_HEADDIM // SSD_PB)
    grp = pl.multiple_of((head // rep) * SSD_STATE, SSD_STATE)
    ela = elat_ref[pl.ds(MISC_DT + head, 1), :]
    btg = bt_ref[pl.ds(grp, SSD_STATE), :]
    ctg = ct_ref[pl.ds(grp, SSD_STATE), :]
    prow = pl.ds(pl.multiple_of(i * SSD_PB, SSD_PB), SSD_PB)
    xrows = xdtt_ref[prow, :]
    p_iota = lax.broadcasted_iota(jnp.int32, (SSD_PB, 1), 0)
    y_rows = jnp.zeros((SSD_PB, NB), F32)
    for pp in range(SSD_PB):
        h_new = ela * ss_ref[0, pp] + xrows[pp:pp + 1, :] * btg
        ss_out[0, pp] = h_new
        y_rows = jnp.where(p_iota == pp, jnp.sum(ctg * h_new, axis=0, keepdims=True), y_rows)
    yt_ref[prow, :] = y_rows

    blk = pl.ds(pl.multiple_of(i * sb, sb), sb)
    vblk = v_ref[blk, :]
    row_iota = lax.broadcasted_iota(jnp.int32, (NB, LANES), 0)
    sub_iota = lax.broadcasted_iota(jnp.int32, (sb, 1), 0)
    o_blk = jnp.zeros((sb, GLA_V), F32)
    for s in range(sb):
        sel = (row_iota == i * sb + s).astype(BF16)
        sel3 = jnp.concatenate([sel] * 3, axis=0)
        qcol = _dot(qt_ref[...], sel)
        kcol = _dot(kt_ref[...], sel)
        ecol = _dot(ebt_ref[...], sel3)
        o_parts = []
        for h in range(GLA_HEADS):
            hd = slice(h * GLA_DK, (h + 1) * GLA_DK)
            s_new = ecol[hd] * sg_ref[s, h] + kcol[hd] * vblk[s:s + 1, h * GLA_DV:(h + 1) * GLA_DV]
            sg_out[s, h] = s_new
            o_parts.append(jnp.sum(qcol[hd] * s_new, axis=0, keepdims=True))
        o_blk = jnp.where(sub_iota == s, jnp.concatenate(o_parts, axis=-1), o_blk)
    o_ref[blk, :] = o_blk

    @pl.when(i == pl.num_programs(0) - 1)
    def _():
        g = seg("g", GLA_V)
        o = _head_rmsnorm(o_ref[...], GLA_DV) * glag_ref[...] * (g * _sigmoid(g))
        z = seg("z", SSD_INNER)
        y = yt_ref[...].T + dsk_ref[...] * xs_ref[...]
        y = _head_rmsnorm(y * (z * _sigmoid(z)), SSD_INNER // SSD_GROUPS) * ssdg_ref[...]
        gate_a = jnp.concatenate([seg("ga0", GA_SPLIT), seg("ga1", D_MODEL - GA_SPLIT)], axis=-1)
        y_ref[:, 0, :] = _merge_out(x_ref[:, 0, :], o.astype(BF16), y.astype(BF16), gate_a, seg("gb", D_MODEL),
                                wbg_ref, wbs_ref, wout_ref, fg_ref)


def _sample_call(x, state_gla, state_ssm, state_conv, params, consts):
    NB = x.shape[0]
    sb = SAMPLE_SB
    steps = NB // sb
    assert NB == LANES and NB % sb == 0 and steps * SSD_PB == SSD_INNER
    assert x.shape == (NB, 1, D_MODEL)
    ss_t = jnp.transpose(state_ssm, (1, 2, 3, 0))
    sc_t = jnp.transpose(state_conv, (1, 0, 2))
    ins = [params[n] for n in _PARAM_ORDER]
    full = lambda a: pl.BlockSpec(a.shape, lambda i: (0,) * a.ndim)
    sg_spec = pl.BlockSpec((sb, GLA_HEADS, GLA_DK, GLA_DV), lambda i: (i, 0, 0, 0))
    pblocks = SSD_HEADDIM // SSD_PB
    ss_spec = pl.BlockSpec((1, SSD_PB, SSD_STATE, NB), lambda i: (i // pblocks, i % pblocks, 0, 0))
    out_shape = (
        jax.ShapeDtypeStruct(x.shape, F32),
        jax.ShapeDtypeStruct(state_gla.shape, F32),
        jax.ShapeDtypeStruct(ss_t.shape, F32),
        jax.ShapeDtypeStruct(sc_t.shape, F32),
    )
    y, sg, ss, sc = pl.pallas_call(
        functools.partial(_sample_kernel, NB),
        grid=(steps,),
        in_specs=[full(x), sg_spec, ss_spec, full(sc_t)] + [full(a) for a in ins],
        out_specs=(full(x), sg_spec, ss_spec, full(sc_t)),
        out_shape=out_shape,
        scratch_shapes=[
            pltpu.VMEM((NG, NB, GW), F32),
            pltpu.VMEM((NB, D_MODEL), BF16),
            pltpu.VMEM((GLA_QK, NB), BF16),
            pltpu.VMEM((GLA_QK, NB), BF16),
            pltpu.VMEM((GLA_QK, 3 * NB), BF16),
            pltpu.VMEM((SSD_INNER, NB), F32),
            pltpu.VMEM((LANES, NB), F32),
            pltpu.VMEM((NB, GLA_V), F32),
            pltpu.VMEM((NB, SSD_INNER), F32),
            pltpu.VMEM((SSD_BC, NB), F32),
            pltpu.VMEM((SSD_BC, NB), F32),
            pltpu.VMEM((NB, GLA_V), F32),
            pltpu.VMEM((SSD_INNER, NB), F32),
        ],
        compiler_params=pltpu.CompilerParams(dimension_semantics=("arbitrary",), vmem_limit_bytes=VMEM_LIMIT),
        name="decode_step",
    )(x, state_gla, ss_t, sc_t, *ins)
    return y, sg, jnp.transpose(ss, (3, 0, 1, 2)), jnp.transpose(sc, (1, 0, 2))


def kernel(x_prompt, x_sample, state_gla, state_ssm, state_conv, norm_g, w_in, w_a2, b_a2, gla_norm_g, conv_w, conv_b,
           dt_bias, a_log, d_skip, ssd_norm_g, w_br_gla, w_br_ssd, w_out, final_norm_g):
    params = _pack_params(norm_g[0], w_in[0], w_a2[0], b_a2[0], gla_norm_g[0], conv_w[0], conv_b[0], dt_bias[0],
                          a_log[0], d_skip[0], ssd_norm_g[0], w_br_gla[0], w_br_ssd[0], w_out[0], final_norm_g)
    consts = _consts()
    y_p, gla_p, ssm_p, conv_p = _prompt_call(x_prompt, params, consts)
    conv_p = conv_p[:, SUBLANES - (CONV_W - 1):, :]
    y_s, gla_s, ssm_s, conv_s = _sample_call(x_sample, state_gla[0], state_ssm[0], state_conv[0], params, consts)
    return (y_p, y_s, gla_p[None], ssm_p[None], conv_p[None], gla_s[None], ssm_s[None], conv_s[None])
```

```python
import functools

import numpy as np
import jax
import jax.numpy as jnp
from jax import lax
from jax.experimental import pallas as pl
from jax.experimental.pallas import tpu as pltpu

F32 = jnp.float32
BF16 = jnp.bfloat16

D_MODEL = 1024
GLA_HEADS = 4
GLA_DK = 64
GLA_DV = 128
GLA_QK = GLA_HEADS * GLA_DK
GLA_V = GLA_HEADS * GLA_DV
GLA_RANK = 16
GLA_GATE_TEMP = 16.0
SSD_HEADS = 8
SSD_HEADDIM = 64
SSD_INNER = SSD_HEADS * SSD_HEADDIM
SSD_GROUPS = 2
SSD_STATE = 64
SSD_BC = SSD_GROUPS * SSD_STATE
CONV_W = 4
SSD_CONV_DIM = SSD_INNER + 2 * SSD_BC
CHUNK = 64
EPS = 1e-6
PROJ_SIZES = (GLA_QK, GLA_QK, GLA_V, GLA_V, GLA_RANK, SSD_INNER, SSD_CONV_DIM, SSD_HEADS, D_MODEL, D_MODEL)

LANES = 128
SUBLANES = 8

NG = 4
GW = 1280
PW = 256
GA_SPLIT = GW - SSD_INNER
SEG = dict(q=(0, 0), k=(0, GLA_QK), v=(0, 2 * GLA_QK), misc=(0, 2 * GLA_QK + GLA_V),
           g=(1, 0), xbc=(1, GLA_V),
           z=(2, 0), ga0=(2, SSD_INNER),
           ga1=(3, 0), gb=(3, D_MODEL - GA_SPLIT))
MISC_DT = GLA_RANK

_OFF = dict(zip(("q", "k", "v", "g", "a_lr", "z", "xbc", "dt", "ga", "gb"),
                (int(o) for o in np.cumsum((0,) + PROJ_SIZES[:-1]))))
_SRC = dict(q=("main", _OFF["q"], GLA_QK), k=("main", _OFF["k"], GLA_QK), v=("main", _OFF["v"], GLA_V),
            misc=("misc", 0, LANES), g=("main", _OFF["g"], GLA_V), xbc=("main", _OFF["xbc"], SSD_CONV_DIM),
            z=("main", _OFF["z"], SSD_INNER), ga0=("gates", 0, GA_SPLIT), ga1=("gates", GA_SPLIT, D_MODEL - GA_SPLIT),
            gb=("gates", D_MODEL, D_MODEL))
PROJ_PW = 2 * PW
_PIECES = [(SEG[n][0], SEG[n][1] + o, min(PROJ_PW, size - o), src, row + o)
           for n, (src, row, size) in _SRC.items() for o in range(0, size, PROJ_PW)]

XH = 2
PROMPT_T = 256
SAMPLE_SB = 16
SSD_PB = 64
VMEM_LIMIT = 56 * 1024 * 1024


def _split3(x):
    hi = x.astype(BF16)
    r1 = x - hi.astype(F32)
    mid = r1.astype(BF16)
    lo = (r1 - mid.astype(F32)).astype(BF16)
    return hi, mid, lo


def _sigmoid(x):
    return 1.0 / (1.0 + jnp.exp(-x))


def _softplus(x):
    return jnp.maximum(x, 0.0) + jnp.log1p(jnp.exp(-jnp.abs(x)))


def _rms_scale(x):
    return lax.rsqrt(jnp.mean(x * x, axis=-1, keepdims=True) + EPS)


def _dot(a, b):
    return jnp.dot(a, b, preferred_element_type=F32)


def _dot_nt(a, b):
    return lax.dot_general(a, b, (((1,), (1,)), ((), ())), preferred_element_type=F32)


def _dot_tn(a, b):
    return lax.dot_general(a, b, (((0,), (0,)), ((), ())), preferred_element_type=F32)


def _consts():
    r = np.arange

    def m(a, dt):
        return jnp.asarray(a.astype(np.float32), dtype=dt)

    c = CHUNK
    return dict(
        tril3=m(r(c)[:, None] >= (r(3 * c)[None, :] % c), BF16),
        km=m(r(GLA_HEADS * c)[:, None] // c == r(GLA_QK)[None, :] // GLA_DK, BF16),
        tril4=m(r(c)[:, None] >= r(GLA_HEADS * c)[None, :] % c, F32),
        vm=m(r(GLA_HEADS * c)[:, None] // c == r(GLA_V)[None, :] // GLA_DV, BF16),
        stm=m(r(GLA_V)[:, None] // GLA_DV == r(GLA_QK)[None, :] // GLA_DK, F32),
        bm=m(r(SSD_HEADS * c)[:, None] // (c * SSD_HEADS // SSD_GROUPS) == r(SSD_BC)[None, :] // SSD_STATE, BF16),
        tril8=m(r(c)[:, None] >= r(SSD_HEADS * c)[None, :] % c, F32),
        diag8=m(r(c)[:, None] == r(SSD_HEADS * c)[None, :] % c, F32),
        xm=m(r(XH * c)[:, None] // c == r(XH * SSD_HEADDIM)[None, :] // SSD_HEADDIM, BF16),
        hm=m(r(SSD_BC)[:, None] // SSD_STATE == r(SSD_INNER)[None, :] // (SSD_INNER // SSD_GROUPS), F32),
        e2=m((r(2 * LANES)[:, None] % LANES) == MISC_DT + r(SSD_INNER)[None, :] // SSD_HEADDIM, BF16),
    )


_CONST_ORDER = ("tril3", "km", "tril4", "vm", "stm", "bm", "tril8", "diag8", "xm", "hm", "e2")


def _pack_params(norm_g, w_in, w_a2, b_a2, gla_norm_g, conv_w, conv_b, dt_bias, a_log, d_skip, ssd_norm_g,
                 w_br_gla, w_br_ssd, w_out, final_norm_g):
    wt = w_in.T
    wm = wt[:_OFF["dt"]].astype(BF16)
    wg = wt[_OFF["ga"]:].astype(BF16)
    wx = jnp.concatenate([wt[_OFF["a_lr"]:_OFF["z"]], wt[_OFF["dt"]:_OFF["ga"]],
                          jnp.zeros((LANES - GLA_RANK - SSD_HEADS, D_MODEL), wt.dtype)], axis=0).astype(BF16)
    wa2 = jnp.concatenate([w_a2, jnp.zeros((LANES - GLA_RANK, GLA_QK), w_a2.dtype)], axis=0).astype(BF16)

    def misc_row(p):
        return jnp.zeros((1, LANES), F32).at[0, MISC_DT:MISC_DT + SSD_HEADS].set(p.astype(F32))

    return dict(
        wm=wm, wg=wg, wx=wx, wa2=wa2,
        wbg=w_br_gla.astype(BF16), wbs=w_br_ssd.astype(BF16), wout=w_out.astype(BF16),
        ng=norm_g.reshape(1, D_MODEL), ba2=b_a2.reshape(1, GLA_QK), glag=gla_norm_g.reshape(1, GLA_V),
        cw=conv_w, cb=conv_b.reshape(1, SSD_CONV_DIM), dtb=misc_row(dt_bias), alog=misc_row(a_log),
        dsk=jnp.repeat(d_skip.astype(F32), SSD_HEADDIM).reshape(1, SSD_INNER),
        ssdg=ssd_norm_g.reshape(1, SSD_INNER), fg=final_norm_g.reshape(1, D_MODEL),
    )


_PARAM_ORDER = ("wm", "wg", "wx", "wa2", "wbg", "wbs", "wout",
                "ng", "ba2", "glag", "cw", "cb", "dtb", "alog", "dsk", "ssdg", "fg")


def _proj_pieces(hn, w_refs, store):
    def piece(gi, l0, width, src, row):
        def emit():
            store(gi, l0, width, _dot_nt(hn(), w_refs[src][row:row + width, :]))
        return emit
    return [piece(*p) for p in _PIECES]


def _a_pad(alog_ref):
    lane = lax.broadcasted_iota(jnp.int32, (1, LANES), 1)
    is_dt = (lane >= MISC_DT) & (lane < MISC_DT + SSD_HEADS)
    return jnp.where(is_dt, -jnp.exp(alog_ref[...]), 0.0)


def _log_sigmoid(x):
    return jnp.minimum(x, 0.0) - jnp.log(1.0 + jnp.exp(-jnp.abs(x)))


def _merge_out(x, o_bf, y_bf, gate_a, gate_b, wbg_ref, wbs_ref, wout_ref, fg_ref):
    merged = _sigmoid(gate_a) * _dot(o_bf, wbg_ref[...]) + _sigmoid(gate_b) * _dot(y_bf, wbs_ref[...])
    out = x + _dot(merged.astype(BF16), wout_ref[...])
    return out * _rms_scale(out) * fg_ref[...]


def _head_rmsnorm(x, width):
    parts = []
    for c0 in range(0, x.shape[-1], width):
        blk = x[:, c0:c0 + width]
        parts.append(blk * _rms_scale(blk))
    return jnp.concatenate(parts, axis=-1)


def _prompt_kernel(T, x_ref, xn_ref, wm_ref, wg_ref, wx_ref, wa2_ref, wbg_ref, wbs_ref, wout_ref,
                   ng_ref, ba2_ref, glag_ref, cw_ref, cb_ref, dtb_ref, alog_ref, dsk_ref, ssdg_ref, fg_ref,
                   tril3_ref, km_ref, tril4_ref, vm_ref, stm_ref, bm_ref, tril8_ref, diag8_ref, xm_ref, hm_ref, e2_ref,
                   y_ref, st_out, hc_out, conv_out,
                   pa_ref, pb_ref, hn_ref, o_buf, y_buf, mg_buf, st_ref, hc_ref):
    t = pl.program_id(1)
    step = pl.program_id(0) * pl.num_programs(1) + t
    c = CHUNK
    tok = slice(SUBLANES, SUBLANES + T)
    last_rows = slice(T, T + SUBLANES)
    gx, lx = SEG["xbc"]
    xbc_l = slice(lx, lx + SSD_CONV_DIM)

    def normalise(src_ref, row0):
        for r0 in range(0, T, c):
            xb = src_ref[row0 + r0:row0 + r0 + c, :]
            hn_ref[r0:r0 + c, :] = (xb * _rms_scale(xb) * ng_ref[...]).astype(BF16)

    w_refs = dict(main=wm_ref, gates=wg_ref, misc=wx_ref)

    def projection(dst_ref):
        def store(gi, l0, width, value):
            dst_ref[gi, tok, l0:l0 + width] = value
        return _proj_pieces(lambda: hn_ref[...], w_refs, store)

    @pl.when(step == 0)
    def _():
        normalise(x_ref, 0)
        for emit in projection(pa_ref):
            emit()

    @pl.when(t == 0)
    def _():
        st_ref[...] = jnp.zeros_like(st_ref)
        hc_ref[...] = jnp.zeros_like(hc_ref)
        pa_ref[gx, 0:SUBLANES, xbc_l] = jnp.zeros((SUBLANES, SSD_CONV_DIM), F32)

    a_pad = _a_pad(alog_ref)

    def run_block(pc_ref, pn_ref, ob, deferred):
      def seg(name, rows, width):
        gi, l0 = SEG[name]
        return pc_ref[gi, rows, l0:l0 + width]


      def stage1(ci):
        r0 = ci * c
        rows = slice(r0 + SUBLANES, r0 + SUBLANES + c)
        misc = seg("misc", rows, LANES)
        pre = _dot(misc.astype(BF16), wa2_ref[...]) + ba2_ref[...]
        dtc = _softplus(misc + dtb_ref[...])
        cum_c = _dot(tril3_ref[...], jnp.concatenate(_split3(dtc * a_pad), axis=0))
        return dict(r0=r0, rows=rows, pre=pre, dtc=dtc, cum_c=cum_c)

      def stage2(d):
        log_a = _log_sigmoid(d["pre"]) * (1.0 / GLA_GATE_TEMP)
        d["b"] = _dot(tril3_ref[...], jnp.concatenate(_split3(log_a), axis=0))
        both = jnp.concatenate([jnp.concatenate(_split3(d["cum_c"])[:2], axis=1),
                                jnp.concatenate(_split3(d["dtc"])[:2], axis=1)], axis=0)
        d["ex"] = _dot(both, e2_ref[...])

      def stage3(d):
        rows, b = d["rows"], d["b"]
        b_last = b[c - 1:c, :]
        k = seg("k", rows, GLA_QK)
        qs = (seg("q", rows, GLA_QK) * (GLA_DK ** -0.5) * jnp.exp(b)).astype(BF16)
        ks = (k * jnp.exp(-b)).astype(BF16)
        kt = jnp.concatenate([ks] * GLA_HEADS, axis=0) * km_ref[...]
        d["a_cat"] = (_dot_nt(qs, kt) * tril4_ref[...]).astype(BF16)
        d["qs"] = qs
        d["kd"] = (k * jnp.exp(b_last - b)).astype(BF16)
        d["eb_last"] = jnp.exp(b_last)

        xe = pc_ref[gx, d["r0"]:d["r0"] + c + SUBLANES, xbc_l]
        x1 = pltpu.roll(xe, 1, 0)
        near = xe[SUBLANES:] * cw_ref[3:4, :] + x1[SUBLANES:] * cw_ref[2:3, :]
        far = pltpu.roll(xe * cw_ref[1:2, :] + x1 * cw_ref[0:1, :], 2, 0)[SUBLANES:]
        xc = near + far + cb_ref[...]
        xc = xc * _sigmoid(xc)
        xs = xc[:, 0:SSD_INNER]
        b_bf = xc[:, SSD_INNER:SSD_INNER + SSD_BC].astype(BF16)
        c_bf = xc[:, SSD_INNER + SSD_BC:].astype(BF16)
        bt = jnp.concatenate([b_bf] * SSD_HEADS, axis=0) * bm_ref[...]
        cum_e = d["ex"][0:c]
        cum_row = jnp.sum(cum_e * diag8_ref[...], axis=0, keepdims=True)
        decay = jnp.exp(jnp.minimum(cum_e - cum_row, 0.0)) * tril8_ref[...]
        d["m_bf"] = (_dot_nt(c_bf, bt) * decay).astype(BF16)
        xdt = xs * d["ex"][c:2 * c]
        c_last = cum_e[c - 1:c, :]
        d["xd"] = (xdt * jnp.exp(c_last - cum_e)).astype(BF16)
        d["xdt_bf"] = xdt.astype(BF16)
        d["ec_last"] = jnp.exp(c_last)
        d["ecum"] = jnp.exp(cum_e)
        d["xs"], d["b_bf"], d["c_bf"] = xs, b_bf, c_bf

      def stage4(d):
        v_bf = seg("v", d["rows"], GLA_V).astype(BF16)
        vbd = jnp.concatenate([v_bf] * GLA_HEADS, axis=0) * vm_ref[...]
        st = st_ref[...]
        d["o"] = _dot(d["a_cat"], vbd) + _dot_nt(d["qs"], st.astype(BF16))
        st_ref[...] = st * d["eb_last"] + _dot_tn(v_bf, d["kd"]) * stm_ref[...]
        ys = []
        for quad in range(SSD_HEADS // XH):
            cols = slice(quad * XH * SSD_HEADDIM, (quad + 1) * XH * SSD_HEADDIM)
            xbd = jnp.concatenate([d["xdt_bf"][:, cols]] * XH, axis=0) * xm_ref[...]
            ys.append(_dot(d["m_bf"][:, cols], xbd))
        hc = hc_ref[...]
        d["y"] = (jnp.concatenate(ys, axis=-1) + _dot(d["c_bf"], hc.astype(BF16)) * d["ecum"]
                  + dsk_ref[...] * d["xs"])
        hc_ref[...] = hc * d["ec_last"] + _dot_tn(d["b_bf"], d["xd"]) * hm_ref[...]

      def stage5(d):
        g = seg("g", d["rows"], GLA_V)
        o = _head_rmsnorm(d["o"], GLA_DV) * glag_ref[...] * (g * _sigmoid(g))
        o_buf[ob, pl.ds(d["r0"], c), :] = o.astype(BF16)
        z = seg("z", d["rows"], SSD_INNER)
        y = _head_rmsnorm(d["y"] * (z * _sigmoid(z)), SSD_INNER // SSD_GROUPS) * ssdg_ref[...]
        y_buf[ob, pl.ds(d["r0"], c), :] = y.astype(BF16)

      fill = projection(pn_ref) + deferred
      nch = T // c
      n_slots = 3 + 2 * nch
      it = iter(fill)

      counts = [0, 1, 1, 1, 1, 1, 1, 2, 2, 2, 1]
      assert sum(counts) == len(fill) and len(counts) == n_slots

      def gap(slot):
        for _ in range(counts[slot]):
            next(it)()

      gap(0)
      ds = [stage1(ci) for ci in range(nch)]
      gap(1)
      for d in ds:
        stage2(d)
      gap(2)
      for k, d in enumerate(ds):
        stage3(d)
        gap(3 + k)
      for k, d in enumerate(ds):
        stage4(d)
        gap(3 + nch + k)
        stage5(d)
      pn_ref[gx, 0:SUBLANES, xbc_l] = pc_ref[gx, last_rows, xbc_l]

    def out_pieces(p_ref, row0, ob):
        prow = slice(SUBLANES, SUBLANES + T)
        xrows = slice(row0, row0 + T)
        nq = D_MODEL // PW

        def gate_cols(name_lo, name_hi, split, j):
            c0 = j * PW
            name, off = (name_lo, c0) if c0 < split else (name_hi, c0 - split)
            gi, l0 = SEG[name]
            return p_ref[gi, prow, l0 + off:l0 + off + PW]

        def merge(j):
            def emit():
                cols = slice(j * PW, (j + 1) * PW)
                ga = gate_cols("ga0", "ga1", GA_SPLIT, j)
                gb = gate_cols("gb", "gb", D_MODEL, j)
                m = (_sigmoid(ga) * _dot(o_buf[ob], wbg_ref[:, cols])
                     + _sigmoid(gb) * _dot(y_buf[ob], wbs_ref[:, cols]))
                mg_buf[:, cols] = m.astype(BF16)
            return emit

        def project(j):
            def emit():
                cols = slice(j * PW, (j + 1) * PW)
                y_ref[xrows, cols] = x_ref[xrows, cols] + _dot(mg_buf[...], wout_ref[:, cols])
            return emit

        def final_norm():
            for r0 in range(row0, row0 + T, c):
                v = y_ref[r0:r0 + c, :]
                y_ref[r0:r0 + c, :] = v * _rms_scale(v) * fg_ref[...]

        return [merge(j) for j in range(nq)] + [project(j) for j in range(nq)] + [final_norm]

    normalise(x_ref, T)
    run_block(pa_ref, pb_ref, 0, [])
    for emit in out_pieces(pa_ref, 0, 0):
        emit()
    normalise(xn_ref, 0)
    run_block(pb_ref, pa_ref, 1, [])
    for emit in out_pieces(pb_ref, T, 1):
        emit()

    @pl.when(t == pl.num_programs(1) - 1)
    def _():
        for h in range(GLA_HEADS):
            blk = st_ref[h * GLA_DV:(h + 1) * GLA_DV, :].T
            st_out[h] = blk[h * GLA_DK:(h + 1) * GLA_DK, :]
        hct = hc_ref[...].T
        rep = SSD_HEADS // SSD_GROUPS
        for h in range(SSD_HEADS):
            blk = hct[h * SSD_HEADDIM:(h + 1) * SSD_HEADDIM, :]
            if h // rep:
                blk = pltpu.roll(blk, SSD_BC - (h // rep) * SSD_STATE, 1)
            hc_out[h] = blk[:, 0:SSD_STATE]
        conv_out[...] = pb_ref[gx, last_rows, xbc_l]


def _prompt_call(x, params, consts):
    B, L, _ = x.shape
    T = PROMPT_T
    assert L % (2 * T) == 0 and T == NG * CHUNK and CONV_W == 4 and SSD_GROUPS == 2
    nt = L // (2 * T)
    full = lambda a: pl.BlockSpec(a.shape, lambda b, t: (0,) * a.ndim)
    ins = [params[n] for n in _PARAM_ORDER] + [consts[n] for n in _CONST_ORDER]
    out_shape = (
        jax.ShapeDtypeStruct((B, L, D_MODEL), F32),
        jax.ShapeDtypeStruct((B, GLA_HEADS, GLA_DK, GLA_DV), F32),
        jax.ShapeDtypeStruct((B, SSD_HEADS, SSD_HEADDIM, SSD_STATE), F32),
        jax.ShapeDtypeStruct((B, SUBLANES, SSD_CONV_DIM), F32),
    )

    def next_block(b, t):
        lin = jnp.minimum(b * nt + t + 1, B * nt - 1)
        return (lin // nt, 2 * (lin % nt), 0)

    tok = pl.BlockSpec((None, 2 * T, D_MODEL), lambda b, t: (b, t, 0))
    tok_next = pl.BlockSpec((None, T, D_MODEL), next_block)
    per_b = lambda s: pl.BlockSpec((None,) + s, lambda b, t: (b,) + (0,) * len(s))
    return pl.pallas_call(
        functools.partial(_prompt_kernel, T),
        grid=(B, nt),
        in_specs=[tok, tok_next] + [full(a) for a in ins],
        out_specs=(tok, per_b((GLA_HEADS, GLA_DK, GLA_DV)), per_b((SSD_HEADS, SSD_HEADDIM, SSD_STATE)),
                   per_b((SUBLANES, SSD_CONV_DIM))),
        out_shape=out_shape,
        scratch_shapes=[
            pltpu.VMEM((NG, T + SUBLANES, GW), F32),
            pltpu.VMEM((NG, T + SUBLANES, GW), F32),
            pltpu.VMEM((T, D_MODEL), BF16),
            pltpu.VMEM((2, T, GLA_V), BF16),
            pltpu.VMEM((2, T, SSD_INNER), BF16),
            pltpu.VMEM((T, D_MODEL), BF16),
            pltpu.VMEM((GLA_V, GLA_QK), F32),
            pltpu.VMEM((SSD_BC, SSD_INNER), F32),
        ],
        compiler_params=pltpu.CompilerParams(dimension_semantics=("arbitrary", "arbitrary"),
                                             vmem_limit_bytes=VMEM_LIMIT),
        name="prompt_step",
    )(x, x, *ins)


def _sample_kernel(NB, x_ref, sg_ref, ss_ref, sc_ref, wm_ref, wg_ref, wx_ref, wa2_ref, wbg_ref, wbs_ref, wout_ref,
                   ng_ref, ba2_ref, glag_ref, cw_ref, cb_ref, dtb_ref, alog_ref, dsk_ref, ssdg_ref, fg_ref,
                   y_ref, sg_out, ss_out, sc_out,
                   p_ref, hn_ref, qt_ref, kt_ref, ebt_ref, xdtt_ref, elat_ref, v_ref, xs_ref, bt_ref, ct_ref,
                   o_ref, yt_ref):
    i = pl.program_id(0)
    sb = SAMPLE_SB
    cd = SSD_CONV_DIM
    rep = SSD_HEADS // SSD_GROUPS

    def seg(name, width):
        gi, l0 = SEG[name]
        return p_ref[gi, :, l0:l0 + width]

    @pl.when(i == 0)
    def _():
        xb = x_ref[:, 0, :]
        hn_ref[...] = (xb * _rms_scale(xb) * ng_ref[...]).astype(BF16)
        def store(gi, l0, width, value):
            p_ref[gi, :, l0:l0 + width] = value
        for emit in _proj_pieces(lambda: hn_ref[...], dict(main=wm_ref, gates=wg_ref, misc=wx_ref), store):
            emit()
        misc = seg("misc", LANES)
        pre = _dot(misc.astype(BF16), wa2_ref[...]) + ba2_ref[...]
        log_a = _log_sigmoid(pre) * (1.0 / GLA_GATE_TEMP)
        qt_ref[...] = (seg("q", GLA_QK) * (GLA_DK ** -0.5)).T.astype(BF16)
        kt_ref[...] = seg("k", GLA_QK).T.astype(BF16)
        ebt_ref[...] = jnp.concatenate(_split3(jnp.exp(log_a).T), axis=1)
        v_ref[...] = seg("v", GLA_V)

        xbc = seg("xbc", cd)
        xc = xbc * cw_ref[CONV_W - 1:CONV_W, :] + cb_ref[...]
        for s in range(CONV_W - 1):
            xc = xc + sc_ref[s] * cw_ref[s:s + 1, :]
        for s in range(CONV_W - 2):
            sc_out[s] = sc_ref[s + 1]
        sc_out[CONV_W - 2] = xbc
        xc = xc * _sigmoid(xc)
        xs = xc[:, 0:SSD_INNER]
        xs_ref[...] = xs
        bt_ref[...] = xc[:, SSD_INNER:SSD_INNER + SSD_BC].T
        ct_ref[...] = xc[:, SSD_INNER + SSD_BC:].T
        dtc = _softplus(misc + dtb_ref[...])
        dtt = dtc.T
        elat_ref[...] = jnp.exp(dtc * _a_pad(alog_ref)).T
        xst = xs.T
        for h in range(SSD_HEADS):
            hp = slice(h * SSD_HEADDIM, (h + 1) * SSD_HEADDIM)
            xdtt_ref[hp, :] = xst[hp, :] * dtt[MISC_DT + h:MISC_DT + h + 1, :]

    head = i // (SSD_HEADDIM // SSD_PB)
    grp = pl.multiple_of((head // rep) * SSD_STATE, SSD_STATE)
    ela = elat_ref[pl.ds(MISC_DT + head, 1), :]
    btg = bt_ref[pl.ds(grp, SSD_STATE), :]
    ctg = ct_ref[pl.ds(grp, SSD_STATE), :]
    prow = pl.ds(pl.multiple_of(i * SSD_PB, SSD_PB), SSD_PB)
    xrows = xdtt_ref[prow, :]
    p_iota = lax.broadcasted_iota(jnp.int32, (SSD_PB, 1), 0)
    y_rows = jnp.zeros((SSD_PB, NB), F32)
    for pp in range(SSD_PB):
        h_new = ela * ss_ref[0, pp] + xrows[pp:pp + 1, :] * btg
        ss_out[0, pp] = h_new
        y_rows = jnp.where(p_iota == pp, jnp.sum(ctg * h_new, axis=0, keepdims=True), y_rows)
    yt_ref[prow, :] = y_rows

    blk = pl.ds(pl.multiple_of(i * sb, sb), sb)
    vblk = v_ref[blk, :]
    row_iota = lax.broadcasted_iota(jnp.int32, (NB, LANES), 0)
    sub_iota = lax.broadcasted_iota(jnp.int32, (sb, 1), 0)
    o_blk = jnp.zeros((sb, GLA_V), F32)
    for s in range(sb):
        sel = (row_iota == i * sb + s).astype(BF16)
        sel3 = jnp.concatenate([sel] * 3, axis=0)
        qcol = _dot(qt_ref[...], sel)
        kcol = _dot(kt_ref[...], sel)
        ecol = _dot(ebt_ref[...], sel3)
        o_parts = []
        for h in range(GLA_HEADS):
            hd = slice(h * GLA_DK, (h + 1) * GLA_DK)
            s_new = ecol[hd] * sg_ref[s, h] + kcol[hd] * vblk[s:s + 1, h * GLA_DV:(h + 1) * GLA_DV]
            sg_out[s, h] = s_new
            o_parts.append(jnp.sum(qcol[hd] * s_new, axis=0, keepdims=True))
        o_blk = jnp.where(sub_iota == s, jnp.concatenate(o_parts, axis=-1), o_blk)
    o_ref[blk, :] = o_blk

    @pl.when(i == pl.num_programs(0) - 1)
    def _():
        g = seg("g", GLA_V)
        o = _head_rmsnorm(o_ref[...], GLA_DV) * glag_ref[...] * (g * _sigmoid(g))
        z = seg("z", SSD_INNER)
        y = yt_ref[...].T + dsk_ref[...] * xs_ref[...]
        y = _head_rmsnorm(y * (z * _sigmoid(z)), SSD_INNER // SSD_GROUPS) * ssdg_ref[...]
        gate_a = jnp.concatenate([seg("ga0", GA_SPLIT), seg("ga1", D_MODEL - GA_SPLIT)], axis=-1)
        y_ref[:, 0, :] = _merge_out(x_ref[:, 0, :], o.astype(BF16), y.astype(BF16), gate_a, seg("gb", D_MODEL),
                                wbg_ref, wbs_ref, wout_ref, fg_ref)


def _sample_call(x, state_gla, state_ssm, state_conv, params, consts):
    NB = x.shape[0]
    sb = SAMPLE_SB
    steps = NB // sb
    assert NB == LANES and NB % sb == 0 and steps * SSD_PB == SSD_INNER
    assert x.shape == (NB, 1, D_MODEL)
    ss_t = jnp.transpose(state_ssm, (1, 2, 3, 0))
    sc_t = jnp.transpose(state_conv, (1, 0, 2))
    ins = [params[n] for n in _PARAM_ORDER]
    full = lambda a: pl.BlockSpec(a.shape, lambda i: (0,) * a.ndim)
    sg_spec = pl.BlockSpec((sb, GLA_HEADS, GLA_DK, GLA_DV), lambda i: (i, 0, 0, 0))
    pblocks = SSD_HEADDIM // SSD_PB
    ss_spec = pl.BlockSpec((1, SSD_PB, SSD_STATE, NB), lambda i: (i // pblocks, i % pblocks, 0, 0))
    out_shape = (
        jax.ShapeDtypeStruct(x.shape, F32),
        jax.ShapeDtypeStruct(state_gla.shape, F32),
        jax.ShapeDtypeStruct(ss_t.shape, F32),
        jax.ShapeDtypeStruct(sc_t.shape, F32),
    )
    y, sg, ss, sc = pl.pallas_call(
        functools.partial(_sample_kernel, NB),
        grid=(steps,),
        in_specs=[full(x), sg_spec, ss_spec, full(sc_t)] + [full(a) for a in ins],
        out_specs=(full(x), sg_spec, ss_spec, full(sc_t)),
        out_shape=out_shape,
        scratch_shapes=[
            pltpu.VMEM((NG, NB, GW), F32),
            pltpu.VMEM((NB, D_MODEL), BF16),
            pltpu.VMEM((GLA_QK, NB), BF16),
            pltpu.VMEM((GLA_QK, NB), BF16),
            pltpu.VMEM((GLA_QK, 3 * NB), BF16),
            pltpu.VMEM((SSD_INNER, NB), F32),
            pltpu.VMEM((LANES, NB), F32),
            pltpu.VMEM((NB, GLA_V), F32),
            pltpu.VMEM((NB, SSD_INNER), F32),
            pltpu.VMEM((SSD_BC, NB), F32),
            pltpu.VMEM((SSD_BC, NB), F32),
            pltpu.VMEM((NB, GLA_V), F32),
            pltpu.VMEM((SSD_INNER, NB), F32),
        ],
        compiler_params=pltpu.CompilerParams(dimension_semantics=("arbitrary",), vmem_limit_bytes=VMEM_LIMIT),
        name="decode_step",
    )(x, state_gla, ss_t, sc_t, *ins)
    return y, sg, jnp.transpose(ss, (3, 0, 1, 2)), jnp.transpose(sc, (1, 0, 2))


def kernel(x_prompt, x_sample, state_gla, state_ssm, state_conv, norm_g, w_in, w_a2, b_a2, gla_norm_g, conv_w, conv_b,
           dt_bias, a_log, d_skip, ssd_norm_g, w_br_gla, w_br_ssd, w_out, final_norm_g):
    params = _pack_params(norm_g[0], w_in[0], w_a2[0], b_a2[0], gla_norm_g[0], conv_w[0], conv_b[0], dt_bias[0],
                          a_log[0], d_skip[0], ssd_norm_g[0], w_br_gla[0], w_br_ssd[0], w_out[0], final_norm_g)
    consts = _consts()
    y_p, gla_p, ssm_p, conv_p = _prompt_call(x_prompt, params, consts)
    conv_p = conv_p[:, SUBLANES - (CONV_W - 1):, :]
    y_s, gla_s, ssm_s, conv_s = _sample_call(x_sample, state_gla[0], state_ssm[0], state_conv[0], params, consts)
    return (y_p, y_s, gla_p[None], ssm_p[None], conv_p[None], gla_s[None], ssm_s[None], conv_s[None])
```

```python
import functools

import numpy as np
import jax
import jax.numpy as jnp
from jax import lax
from jax.experimental import pallas as pl
from jax.experimental.pallas import tpu as pltpu

F32 = jnp.float32
BF16 = jnp.bfloat16

D_MODEL = 1024
GLA_HEADS = 4
GLA_DK = 64
GLA_DV = 128
GLA_QK = GLA_HEADS * GLA_DK
GLA_V = GLA_HEADS * GLA_DV
GLA_RANK = 16
GLA_GATE_TEMP = 16.0
SSD_HEADS = 8
SSD_HEADDIM = 64
SSD_INNER = SSD_HEADS * SSD_HEADDIM
SSD_GROUPS = 2
SSD_STATE = 64
SSD_BC = SSD_GROUPS * SSD_STATE
CONV_W = 4
SSD_CONV_DIM = SSD_INNER + 2 * SSD_BC
CHUNK = 64
EPS = 1e-6
PROJ_SIZES = (GLA_QK, GLA_QK, GLA_V, GLA_V, GLA_RANK, SSD_INNER, SSD_CONV_DIM, SSD_HEADS, D_MODEL, D_MODEL)

LANES = 128
SUBLANES = 8

NG = 4
GW = 1280
PW = 256
GA_SPLIT = GW - SSD_INNER
SEG = dict(q=(0, 0), k=(0, GLA_QK), v=(0, 2 * GLA_QK), misc=(0, 2 * GLA_QK + GLA_V),
           g=(1, 0), xbc=(1, GLA_V),
           z=(2, 0), ga0=(2, SSD_INNER),
           ga1=(3, 0), gb=(3, D_MODEL - GA_SPLIT))
MISC_DT = GLA_RANK

_OFF = dict(zip(("q", "k", "v", "g", "a_lr", "z", "xbc", "dt", "ga", "gb"),
                (int(o) for o in np.cumsum((0,) + PROJ_SIZES[:-1]))))
_SRC = dict(q=("main", _OFF["q"], GLA_QK), k=("main", _OFF["k"], GLA_QK), v=("main", _OFF["v"], GLA_V),
            misc=("misc", 0, LANES), g=("main", _OFF["g"], GLA_V), xbc=("main", _OFF["xbc"], SSD_CONV_DIM),
            z=("main", _OFF["z"], SSD_INNER), ga0=("gates", 0, GA_SPLIT), ga1=("gates", GA_SPLIT, D_MODEL - GA_SPLIT),
            gb=("gates", D_MODEL, D_MODEL))
PROJ_PW = 4 * PW
_PIECES = [(SEG[n][0], SEG[n][1] + o, min(PROJ_PW, size - o), src, row + o)
           for n, (src, row, size) in _SRC.items() for o in range(0, size, PROJ_PW)]

XH = 2
PROMPT_T = 256
SAMPLE_SB = 16
SSD_PB = 64
VMEM_LIMIT = 56 * 1024 * 1024


def _split3(x):
    hi = x.astype(BF16)
    r1 = x - hi.astype(F32)
    mid = r1.astype(BF16)
    lo = (r1 - mid.astype(F32)).astype(BF16)
    return hi, mid, lo


def _sigmoid(x):
    return 1.0 / (1.0 + jnp.exp(-x))


def _softplus(x):
    return jnp.maximum(x, 0.0) + jnp.log1p(jnp.exp(-jnp.abs(x)))


def _rms_scale(x):
    return lax.rsqrt(jnp.mean(x * x, axis=-1, keepdims=True) + EPS)


def _dot(a, b):
    return jnp.dot(a, b, preferred_element_type=F32)


def _dot_nt(a, b):
    return lax.dot_general(a, b, (((1,), (1,)), ((), ())), preferred_element_type=F32)


def _dot_tn(a, b):
    return lax.dot_general(a, b, (((0,), (0,)), ((), ())), preferred_element_type=F32)


def _consts():
    r = np.arange

    def m(a, dt):
        return jnp.asarray(a.astype(np.float32), dtype=dt)

    c = CHUNK
    return dict(
        tril3=m(r(c)[:, None] >= (r(3 * c)[None, :] % c), BF16),
        km=m(r(GLA_HEADS * c)[:, None] // c == r(GLA_QK)[None, :] // GLA_DK, BF16),
        tril4=m(r(c)[:, None] >= r(GLA_HEADS * c)[None, :] % c, F32),
        vm=m(r(GLA_HEADS * c)[:, None] // c == r(GLA_V)[None, :] // GLA_DV, BF16),
        stm=m(r(GLA_V)[:, None] // GLA_DV == r(GLA_QK)[None, :] // GLA_DK, F32),
        bm=m(r(SSD_HEADS * c)[:, None] // (c * SSD_HEADS // SSD_GROUPS) == r(SSD_BC)[None, :] // SSD_STATE, BF16),
        tril8=m(r(c)[:, None] >= r(SSD_HEADS * c)[None, :] % c, F32),
        diag8=m(r(c)[:, None] == r(SSD_HEADS * c)[None, :] % c, F32),
        xm=m(r(XH * c)[:, None] // c == r(XH * SSD_HEADDIM)[None, :] // SSD_HEADDIM, BF16),
        hm=m(r(SSD_BC)[:, None] // SSD_STATE == r(SSD_INNER)[None, :] // (SSD_INNER // SSD_GROUPS), F32),
        e2=m((r(2 * LANES)[:, None] % LANES) == MISC_DT + r(SSD_INNER)[None, :] // SSD_HEADDIM, BF16),
    )


_CONST_ORDER = ("tril3", "km", "tril4", "vm", "stm", "bm", "tril8", "diag8", "xm", "hm", "e2")


def _pack_params(norm_g, w_in, w_a2, b_a2, gla_norm_g, conv_w, conv_b, dt_bias, a_log, d_skip, ssd_norm_g,
                 w_br_gla, w_br_ssd, w_out, final_norm_g):
    wt = w_in.T
    wm = wt[:_OFF["dt"]].astype(BF16)
    wg = wt[_OFF["ga"]:].astype(BF16)
    wx = jnp.concatenate([wt[_OFF["a_lr"]:_OFF["z"]], wt[_OFF["dt"]:_OFF["ga"]],
                          jnp.zeros((LANES - GLA_RANK - SSD_HEADS, D_MODEL), wt.dtype)], axis=0).astype(BF16)
    wa2 = jnp.concatenate([w_a2, jnp.zeros((LANES - GLA_RANK, GLA_QK), w_a2.dtype)], axis=0).astype(BF16)

    def misc_row(p):
        return jnp.zeros((1, LANES), F32).at[0, MISC_DT:MISC_DT + SSD_HEADS].set(p.astype(F32))

    return dict(
        wm=wm, wg=wg, wx=wx, wa2=wa2,
        wbg=w_br_gla.astype(BF16), wbs=w_br_ssd.astype(BF16), wout=w_out.astype(BF16),
        ng=norm_g.reshape(1, D_MODEL), ba2=b_a2.reshape(1, GLA_QK), glag=gla_norm_g.reshape(1, GLA_V),
        cw=conv_w, cb=conv_b.reshape(1, SSD_CONV_DIM), dtb=misc_row(dt_bias), alog=misc_row(a_log),
        dsk=jnp.repeat(d_skip.astype(F32), SSD_HEADDIM).reshape(1, SSD_INNER),
        ssdg=ssd_norm_g.reshape(1, SSD_INNER), fg=final_norm_g.reshape(1, D_MODEL),
    )


_PARAM_ORDER = ("wm", "wg", "wx", "wa2", "wbg", "wbs", "wout",
                "ng", "ba2", "glag", "cw", "cb", "dtb", "alog", "dsk", "ssdg", "fg")


def _proj_pieces(hn, w_refs, store):
    def piece(gi, l0, width, src, row):
        def emit():
            store(gi, l0, width, _dot_nt(hn(), w_refs[src][row:row + width, :]))
        return emit
    return [piece(*p) for p in _PIECES]


def _a_pad(alog_ref):
    lane = lax.broadcasted_iota(jnp.int32, (1, LANES), 1)
    is_dt = (lane >= MISC_DT) & (lane < MISC_DT + SSD_HEADS)
    return jnp.where(is_dt, -jnp.exp(alog_ref[...]), 0.0)


def _log_sigmoid(x):
    return jnp.minimum(x, 0.0) - jnp.log(1.0 + jnp.exp(-jnp.abs(x)))


def _merge_out(x, o_bf, y_bf, gate_a, gate_b, wbg_ref, wbs_ref, wout_ref, fg_ref):
    merged = _sigmoid(gate_a) * _dot(o_bf, wbg_ref[...]) + _sigmoid(gate_b) * _dot(y_bf, wbs_ref[...])
    out = x + _dot(merged.astype(BF16), wout_ref[...])
    return out * _rms_scale(out) * fg_ref[...]


def _head_rmsnorm(x, width):
    parts = []
    for c0 in range(0, x.shape[-1], width):
        blk = x[:, c0:c0 + width]
        parts.append(blk * _rms_scale(blk))
    return jnp.concatenate(parts, axis=-1)


def _prompt_kernel(T, x_ref, xn_ref, wm_ref, wg_ref, wx_ref, wa2_ref, wbg_ref, wbs_ref, wout_ref,
                   ng_ref, ba2_ref, glag_ref, cw_ref, cb_ref, dtb_ref, alog_ref, dsk_ref, ssdg_ref, fg_ref,
                   tril3_ref, km_ref, tril4_ref, vm_ref, stm_ref, bm_ref, tril8_ref, diag8_ref, xm_ref, hm_ref, e2_ref,
                   y_ref, st_out, hc_out, conv_out,
                   pa_ref, pb_ref, hn_ref, o_buf, y_buf, mg_buf, st_ref, hc_ref):
    t = pl.program_id(1)
    step = pl.program_id(0) * pl.num_programs(1) + t
    c = CHUNK
    tok = slice(SUBLANES, SUBLANES + T)
    last_rows = slice(T, T + SUBLANES)
    gx, lx = SEG["xbc"]
    xbc_l = slice(lx, lx + SSD_CONV_DIM)

    def normalise(src_ref, row0):
        for r0 in range(0, T, c):
            xb = src_ref[row0 + r0:row0 + r0 + c, :]
            hn_ref[r0:r0 + c, :] = (xb * _rms_scale(xb) * ng_ref[...]).astype(BF16)

    w_refs = dict(main=wm_ref, gates=wg_ref, misc=wx_ref)

    def projection(dst_ref):
        def store(gi, l0, width, value):
            dst_ref[gi, tok, l0:l0 + width] = value
        return _proj_pieces(lambda: hn_ref[...], w_refs, store)

    @pl.when(step == 0)
    def _():
        normalise(x_ref, 0)
        for emit in projection(pa_ref):
            emit()

    @pl.when(t == 0)
    def _():
        st_ref[...] = jnp.zeros_like(st_ref)
        hc_ref[...] = jnp.zeros_like(hc_ref)
        pa_ref[gx, 0:SUBLANES, xbc_l] = jnp.zeros((SUBLANES, SSD_CONV_DIM), F32)

    a_pad = _a_pad(alog_ref)

    def run_block(pc_ref, pn_ref, ob, deferred):
      def seg(name, rows, width):
        gi, l0 = SEG[name]
        return pc_ref[gi, rows, l0:l0 + width]


      def stage1(ci):
        r0 = ci * c
        rows = slice(r0 + SUBLANES, r0 + SUBLANES + c)
        misc = seg("misc", rows, LANES)
        pre = _dot(misc.astype(BF16), wa2_ref[...]) + ba2_ref[...]
        dtc = _softplus(misc + dtb_ref[...])
        cum_c = _dot(tril3_ref[...], jnp.concatenate(_split3(dtc * a_pad), axis=0))
        return dict(r0=r0, rows=rows, pre=pre, dtc=dtc, cum_c=cum_c)

      def stage2(d):
        log_a = _log_sigmoid(d["pre"]) * (1.0 / GLA_GATE_TEMP)
        d["b"] = _dot(tril3_ref[...], jnp.concatenate(_split3(log_a), axis=0))
        both = jnp.concatenate([jnp.concatenate(_split3(d["cum_c"])[:2], axis=1),
                                jnp.concatenate(_split3(d["dtc"])[:2], axis=1)], axis=0)
        d["ex"] = _dot(both, e2_ref[...])

      def stage3(d):
        rows, b = d["rows"], d["b"]
        b_last = b[c - 1:c, :]
        k = seg("k", rows, GLA_QK)
        qs = (seg("q", rows, GLA_QK) * (GLA_DK ** -0.5) * jnp.exp(b)).astype(BF16)
        ks = (k * jnp.exp(-b)).astype(BF16)
        kt = jnp.concatenate([ks] * GLA_HEADS, axis=0) * km_ref[...]
        d["a_cat"] = (_dot_nt(qs, kt) * tril4_ref[...]).astype(BF16)
        d["qs"] = qs
        d["kd"] = (k * jnp.exp(b_last - b)).astype(BF16)
        d["eb_last"] = jnp.exp(b_last)

        xe = pc_ref[gx, d["r0"]:d["r0"] + c + SUBLANES, xbc_l]
        x1 = pltpu.roll(xe, 1, 0)
        near = xe[SUBLANES:] * cw_ref[3:4, :] + x1[SUBLANES:] * cw_ref[2:3, :]
        far = pltpu.roll(xe * cw_ref[1:2, :] + x1 * cw_ref[0:1, :], 2, 0)[SUBLANES:]
        xc = near + far + cb_ref[...]
        xc = xc * _sigmoid(xc)
        xs = xc[:, 0:SSD_INNER]
        b_bf = xc[:, SSD_INNER:SSD_INNER + SSD_BC].astype(BF16)
        c_bf = xc[:, SSD_INNER + SSD_BC:].astype(BF16)
        bt = jnp.concatenate([b_bf] * SSD_HEADS, axis=0) * bm_ref[...]
        cum_e = d["ex"][0:c]
        cum_row = jnp.sum(cum_e * diag8_ref[...], axis=0, keepdims=True)
        decay = jnp.exp(jnp.minimum(cum_e - cum_row, 0.0)) * tril8_ref[...]
        d["m_bf"] = (_dot_nt(c_bf, bt) * decay).astype(BF16)
        xdt = xs * d["ex"][c:2 * c]
        c_last = cum_e[c - 1:c, :]
        d["xd"] = (xdt * jnp.exp(c_last - cum_e)).astype(BF16)
        d["xdt_bf"] = xdt.astype(BF16)
        d["ec_last"] = jnp.exp(c_last)
        d["ecum"] = jnp.exp(cum_e)
        d["xs"], d["b_bf"], d["c_bf"] = xs, b_bf, c_bf

      def stage4(d):
        v_bf = seg("v", d["rows"], GLA_V).astype(BF16)
        vbd = jnp.concatenate([v_bf] * GLA_HEADS, axis=0) * vm_ref[...]
        st = st_ref[...]
        d["o"] = _dot(d["a_cat"], vbd) + _dot_nt(d["qs"], st.astype(BF16))
        st_ref[...] = st * d["eb_last"] + _dot_tn(v_bf, d["kd"]) * stm_ref[...]
        ys = []
        for quad in range(SSD_HEADS // XH):
            cols = slice(quad * XH * SSD_HEADDIM, (quad + 1) * XH * SSD_HEADDIM)
            xbd = jnp.concatenate([d["xdt_bf"][:, cols]] * XH, axis=0) * xm_ref[...]
            ys.append(_dot(d["m_bf"][:, cols], xbd))
        hc = hc_ref[...]
        d["y"] = (jnp.concatenate(ys, axis=-1) + _dot(d["c_bf"], hc.astype(BF16)) * d["ecum"]
                  + dsk_ref[...] * d["xs"])
        hc_ref[...] = hc * d["ec_last"] + _dot_tn(d["b_bf"], d["xd"]) * hm_ref[...]

      def stage5(d):
        g = seg("g", d["rows"], GLA_V)
        o = _head_rmsnorm(d["o"], GLA_DV) * glag_ref[...] * (g * _sigmoid(g))
        o_buf[ob, pl.ds(d["r0"], c), :] = o.astype(BF16)
        z = seg("z", d["rows"], SSD_INNER)
        y = _head_rmsnorm(d["y"] * (z * _sigmoid(z)), SSD_INNER // SSD_GROUPS) * ssdg_ref[...]
        y_buf[ob, pl.ds(d["r0"], c), :] = y.astype(BF16)

      fill = projection(pn_ref) + deferred
      nch = T // c
      n_slots = 3 + 2 * nch
      it = iter(fill)

      counts = [0, 1, 1, 1, 1, 1, 1, 1, 1, 1, 1]
      assert sum(counts) == len(fill) and len(counts) == n_slots

      def gap(slot):
        for _ in range(counts[slot]):
            next(it)()

      gap(0)
      ds = [stage1(ci) for ci in range(nch)]
      gap(1)
      for d in ds:
        stage2(d)
      gap(2)
      for k, d in enumerate(ds):
        stage3(d)
        gap(3 + k)
      for k, d in enumerate(ds):
        stage4(d)
        gap(3 + nch + k)
        stage5(d)
      pn_ref[gx, 0:SUBLANES, xbc_l] = pc_ref[gx, last_rows, xbc_l]

    def out_pieces(p_ref, row0, ob):
        prow = slice(SUBLANES, SUBLANES + T)
        xrows = slice(row0, row0 + T)
        nq = D_MODEL // PW

        def gate_cols(name_lo, name_hi, split, j):
            c0 = j * PW
            name, off = (name_lo, c0) if c0 < split else (name_hi, c0 - split)
            gi, l0 = SEG[name]
            return p_ref[gi, prow, l0 + off:l0 + off + PW]

        def merge(j):
            def emit():
                cols = slice(j * PW, (j + 1) * PW)
                ga = gate_cols("ga0", "ga1", GA_SPLIT, j)
                gb = gate_cols("gb", "gb", D_MODEL, j)
                m = (_sigmoid(ga) * _dot(o_buf[ob], wbg_ref[:, cols])
                     + _sigmoid(gb) * _dot(y_buf[ob], wbs_ref[:, cols]))
                mg_buf[:, cols] = m.astype(BF16)
            return emit

        def project(j):
            def emit():
                cols = slice(j * PW, (j + 1) * PW)
                y_ref[xrows, cols] = x_ref[xrows, cols] + _dot(mg_buf[...], wout_ref[:, cols])
            return emit

        def final_norm():
            for r0 in range(row0, row0 + T, c):
                v = y_ref[r0:r0 + c, :]
                y_ref[r0:r0 + c, :] = v * _rms_scale(v) * fg_ref[...]

        return [merge(j) for j in range(nq)] + [project(j) for j in range(nq)] + [final_norm]

    normalise(x_ref, T)
    run_block(pa_ref, pb_ref, 0, [])
    for emit in out_pieces(pa_ref, 0, 0):
        emit()
    normalise(xn_ref, 0)
    run_block(pb_ref, pa_ref, 1, [])
    for emit in out_pieces(pb_ref, T, 1):
        emit()

    @pl.when(t == pl.num_programs(1) - 1)
    def _():
        for h in range(GLA_HEADS):
            blk = st_ref[h * GLA_DV:(h + 1) * GLA_DV, :].T
            st_out[h] = blk[h * GLA_DK:(h + 1) * GLA_DK, :]
        hct = hc_ref[...].T
        rep = SSD_HEADS // SSD_GROUPS
        for h in range(SSD_HEADS):
            blk = hct[h * SSD_HEADDIM:(h + 1) * SSD_HEADDIM, :]
            if h // rep:
                blk = pltpu.roll(blk, SSD_BC - (h // rep) * SSD_STATE, 1)
            hc_out[h] = blk[:, 0:SSD_STATE]
        conv_out[...] = pb_ref[gx, last_rows, xbc_l]


def _prompt_call(x, params, consts):
    B, L, _ = x.shape
    T = PROMPT_T
    assert L % (2 * T) == 0 and T == NG * CHUNK and CONV_W == 4 and SSD_GROUPS == 2
    nt = L // (2 * T)
    full = lambda a: pl.BlockSpec(a.shape, lambda b, t: (0,) * a.ndim)
    ins = [params[n] for n in _PARAM_ORDER] + [consts[n] for n in _CONST_ORDER]
    out_shape = (
        jax.ShapeDtypeStruct((B, L, D_MODEL), F32),
        jax.ShapeDtypeStruct((B, GLA_HEADS, GLA_DK, GLA_DV), F32),
        jax.ShapeDtypeStruct((B, SSD_HEADS, SSD_HEADDIM, SSD_STATE), F32),
        jax.ShapeDtypeStruct((B, SUBLANES, SSD_CONV_DIM), F32),
    )

    def next_block(b, t):
        lin = jnp.minimum(b * nt + t + 1, B * nt - 1)
        return (lin // nt, 2 * (lin % nt), 0)

    tok = pl.BlockSpec((None, 2 * T, D_MODEL), lambda b, t: (b, t, 0))
    tok_next = pl.BlockSpec((None, T, D_MODEL), next_block)
    per_b = lambda s: pl.BlockSpec((None,) + s, lambda b, t: (b,) + (0,) * len(s))
    return pl.pallas_call(
        functools.partial(_prompt_kernel, T),
        grid=(B, nt),
        in_specs=[tok, tok_next] + [full(a) for a in ins],
        out_specs=(tok, per_b((GLA_HEADS, GLA_DK, GLA_DV)), per_b((SSD_HEADS, SSD_HEADDIM, SSD_STATE)),
                   per_b((SUBLANES, SSD_CONV_DIM))),
        out_shape=out_shape,
        scratch_shapes=[
            pltpu.VMEM((NG, T + SUBLANES, GW), F32),
            pltpu.VMEM((NG, T + SUBLANES, GW), F32),
            pltpu.VMEM((T, D_MODEL), BF16),
            pltpu.VMEM((2, T, GLA_V), BF16),
            pltpu.VMEM((2, T, SSD_INNER), BF16),
            pltpu.VMEM((T, D_MODEL), BF16),
            pltpu.VMEM((GLA_V, GLA_QK), F32),
            pltpu.VMEM((SSD_BC, SSD_INNER), F32),
        ],
        compiler_params=pltpu.CompilerParams(dimension_semantics=("arbitrary", "arbitrary"),
                                             vmem_limit_bytes=VMEM_LIMIT),
        name="prompt_step",
    )(x, x, *ins)


def _sample_kernel(NB, x_ref, sg_ref, ss_ref, sc_ref, wm_ref, wg_ref, wx_ref, wa2_ref, wbg_ref, wbs_ref, wout_ref,
                   ng_ref, ba2_ref, glag_ref, cw_ref, cb_ref, dtb_ref, alog_ref, dsk_ref, ssdg_ref, fg_ref,
                   y_ref, sg_out, ss_out, sc_out,
                   p_ref, hn_ref, qt_ref, kt_ref, ebt_ref, xdtt_ref, elat_ref, v_ref, xs_ref, bt_ref, ct_ref,
                   o_ref, yt_ref):
    i = pl.program_id(0)
    sb = SAMPLE_SB
    cd = SSD_CONV_DIM
    rep = SSD_HEADS // SSD_GROUPS

    def seg(name, width):
        gi, l0 = SEG[name]
        return p_ref[gi, :, l0:l0 + width]

    @pl.when(i == 0)
    def _():
        xb = x_ref[:, 0, :]
        hn_ref[...] = (xb * _rms_scale(xb) * ng_ref[...]).astype(BF16)
        def store(gi, l0, width, value):
            p_ref[gi, :, l0:l0 + width] = value
        for emit in _proj_pieces(lambda: hn_ref[...], dict(main=wm_ref, gates=wg_ref, misc=wx_ref), store):
            emit()
        misc = seg("misc", LANES)
        pre = _dot(misc.astype(BF16), wa2_ref[...]) + ba2_ref[...]
        log_a = _log_sigmoid(pre) * (1.0 / GLA_GATE_TEMP)
        qt_ref[...] = (seg("q", GLA_QK) * (GLA_DK ** -0.5)).T.astype(BF16)
        kt_ref[...] = seg("k", GLA_QK).T.astype(BF16)
        ebt_ref[...] = jnp.concatenate(_split3(jnp.exp(log_a).T), axis=1)
        v_ref[...] = seg("v", GLA_V)

        xbc = seg("xbc", cd)
        xc = xbc * cw_ref[CONV_W - 1:CONV_W, :] + cb_ref[...]
        for s in range(CONV_W - 1):
            xc = xc + sc_ref[s] * cw_ref[s:s + 1, :]
        for s in range(CONV_W - 2):
            sc_out[s] = sc_ref[s + 1]
        sc_out[CONV_W - 2] = xbc
        xc = xc * _sigmoid(xc)
        xs = xc[:, 0:SSD_INNER]
        xs_ref[...] = xs
        bt_ref[...] = xc[:, SSD_INNER:SSD_INNER + SSD_BC].T
        ct_ref[...] = xc[:, SSD_INNER + SSD_BC:].T
        dtc = _softplus(misc + dtb_ref[...])
        dtt = dtc.T
        elat_ref[...] = jnp.exp(dtc * _a_pad(alog_ref)).T
        xst = xs.T
        for h in range(SSD_HEADS):
            hp = slice(h * SSD_HEADDIM, (h + 1) * SSD_HEADDIM)
            xdtt_ref[hp, :] = xst[hp, :] * dtt[MISC_DT + h:MISC_DT + h + 1, :]

    head = i // (SSD_HEADDIM // SSD_PB)
    grp = pl.multiple_of((head // rep) * SSD_STATE, SSD_STATE)
    ela = elat_ref[pl.ds(MISC_DT + head, 1), :]
    btg = bt_ref[pl.ds(grp, SSD_STATE), :]
    ctg = ct_ref[pl.ds(grp, SSD_STATE), :]
    prow = pl.ds(pl.multiple_of(i * SSD_PB, SSD_PB), SSD_PB)
    xrows = xdtt_ref[prow, :]
    p_iota = lax.broadcasted_iota(jnp.int32, (SSD_PB, 1), 0)
    y_rows = jnp.zeros((SSD_PB, NB), F32)
    for pp in range(SSD_PB):
        h_new = ela * ss_ref[0, pp] + xrows[pp:pp + 1, :] * btg
        ss_out[0, pp] = h_new
        y_rows = jnp.where(p_iota == pp, jnp.sum(ctg * h_new, axis=0, keepdims=True), y_rows)
    yt_ref[prow, :] = y_rows

    blk = pl.ds(pl.multiple_of(i * sb, sb), sb)
    vblk = v_ref[blk, :]
    row_iota = lax.broadcasted_iota(jnp.int32, (NB, LANES), 0)
    sub_iota = lax.broadcasted_iota(jnp.int32, (sb, 1), 0)
    o_blk = jnp.zeros((sb, GLA_V), F32)
    for s in range(sb):
        sel = (row_iota == i * sb + s).astype(BF16)
        sel3 = jnp.concatenate([sel] * 3, axis=0)
        qcol = _dot(qt_ref[...], sel)
        kcol = _dot(kt_ref[...], sel)
        ecol = _dot(ebt_ref[...], sel3)
        o_parts = []
        for h in range(GLA_HEADS):
            hd = slice(h * GLA_DK, (h + 1) * GLA_DK)
            s_new = ecol[hd] * sg_ref[s, h] + kcol[hd] * vblk[s:s + 1, h * GLA_DV:(h + 1) * GLA_DV]
            sg_out[s, h] = s_new
            o_parts.append(jnp.sum(qcol[hd] * s_new, axis=0, keepdims=True))
        o_blk = jnp.where(sub_iota == s, jnp.concatenate(o_parts, axis=-1), o_blk)
    o_ref[blk, :] = o_blk

    @pl.when(i == pl.num_programs(0) - 1)
    def _():
        g = seg("g", GLA_V)
        o = _head_rmsnorm(o_ref[...], GLA_DV) * glag_ref[...] * (g * _sigmoid(g))
        z = seg("z", SSD_INNER)
        y = yt_ref[...].T + dsk_ref[...] * xs_ref[...]
        y = _head_rmsnorm(y * (z * _sigmoid(z)), SSD_INNER // SSD_GROUPS) * ssdg_ref[...]
        gate_a = jnp.concatenate([seg("ga0", GA_SPLIT), seg("ga1", D_MODEL - GA_SPLIT)], axis=-1)
        y_ref[:, 0, :] = _merge_out(x_ref[:, 0, :], o.astype(BF16), y.astype(BF16), gate_a, seg("gb", D_MODEL),
                                wbg_ref, wbs_ref, wout_ref, fg_ref)


def _sample_call(x, state_gla, state_ssm, state_conv, params, consts):
    NB = x.shape[0]
    sb = SAMPLE_SB
    steps = NB // sb
    assert NB == LANES and NB % sb == 0 and steps * SSD_PB == SSD_INNER
    assert x.shape == (NB, 1, D_MODEL)
    ss_t = jnp.transpose(state_ssm, (1, 2, 3, 0))
    sc_t = jnp.transpose(state_conv, (1, 0, 2))
    ins = [params[n] for n in _PARAM_ORDER]
    full = lambda a: pl.BlockSpec(a.shape, lambda i: (0,) * a.ndim)
    sg_spec = pl.BlockSpec((sb, GLA_HEADS, GLA_DK, GLA_DV), lambda i: (i, 0, 0, 0))
    pblocks = SSD_HEADDIM // SSD_PB
    ss_spec = pl.BlockSpec((1, SSD_PB, SSD_STATE, NB), lambda i: (i // pblocks, i % pblocks, 0, 0))
    out_shape = (
        jax.ShapeDtypeStruct(x.shape, F32),
        jax.ShapeDtypeStruct(state_gla.shape, F32),
        jax.ShapeDtypeStruct(ss_t.shape, F32),
        jax.ShapeDtypeStruct(sc_t.shape, F32),
    )
    y, sg, ss, sc = pl.pallas_call(
        functools.partial(_sample_kernel, NB),
        grid=(steps,),
        in_specs=[full(x), sg_spec, ss_spec, full(sc_t)] + [full(a) for a in ins],
        out_specs=(full(x), sg_spec, ss_spec, full(sc_t)),
        out_shape=out_shape,
        scratch_shapes=[
            pltpu.VMEM((NG, NB, GW), F32),
            pltpu.VMEM((NB, D_MODEL), BF16),
            pltpu.VMEM((GLA_QK, NB), BF16),
            pltpu.VMEM((GLA_QK, NB), BF16),
            pltpu.VMEM((GLA_QK, 3 * NB), BF16),
            pltpu.VMEM((SSD_INNER, NB), F32),
            pltpu.VMEM((LANES, NB), F32),
            pltpu.VMEM((NB, GLA_V), F32),
            pltpu.VMEM((NB, SSD_INNER), F32),
            pltpu.VMEM((SSD_BC, NB), F32),
            pltpu.VMEM((SSD_BC, NB), F32),
            pltpu.VMEM((NB, GLA_V), F32),
            pltpu.VMEM((SSD_INNER, NB), F32),
        ],
        compiler_params=pltpu.CompilerParams(dimension_semantics=("arbitrary",), vmem_limit_bytes=VMEM_LIMIT),
        name="decode_step",
    )(x, state_gla, ss_t, sc_t, *ins)
    return y, sg, jnp.transpose(ss, (3, 0, 1, 2)), jnp.transpose(sc, (1, 0, 2))


def kernel(x_prompt, x_sample, state_gla, state_ssm, state_conv, norm_g, w_in, w_a2, b_a2, gla_norm_g, conv_w, conv_b,
           dt_bias, a_log, d_skip, ssd_norm_g, w_br_gla, w_br_ssd, w_out, final_norm_g):
    params = _pack_params(norm_g[0], w_in[0], w_a2[0], b_a2[0], gla_norm_g[0], conv_w[0], conv_b[0], dt_bias[0],
                          a_log[0], d_skip[0], ssd_norm_g[0], w_br_gla[0], w_br_ssd[0], w_out[0], final_norm_g)
    consts = _consts()
    y_p, gla_p, ssm_p, conv_p = _prompt_call(x_prompt, params, consts)
    conv_p = conv_p[:, SUBLANES - (CONV_W - 1):, :]
    y_s, gla_s, ssm_s, conv_s = _sample_call(x_sample, state_gla[0], state_ssm[0], state_conv[0], params, consts)
    return (y_p, y_s, gla_p[None], ssm_p[None], conv_p[None], gla_s[None], ssm_s[None], conv_s[None])
```

```python
import functools

import numpy as np
import jax
import jax.numpy as jnp
from jax import lax
from jax.experimental import pallas as pl
from jax.experimental.pallas import tpu as pltpu

F32 = jnp.float32
BF16 = jnp.bfloat16

D_MODEL = 1024
GLA_HEADS = 4
GLA_DK = 64
GLA_DV = 128
GLA_QK = GLA_HEADS * GLA_DK
GLA_V = GLA_HEADS * GLA_DV
GLA_RANK = 16
GLA_GATE_TEMP = 16.0
SSD_HEADS = 8
SSD_HEADDIM = 64
SSD_INNER = SSD_HEADS * SSD_HEADDIM
SSD_GROUPS = 2
SSD_STATE = 64
SSD_BC = SSD_GROUPS * SSD_STATE
CONV_W = 4
SSD_CONV_DIM = SSD_INNER + 2 * SSD_BC
CHUNK = 64
EPS = 1e-6
PROJ_SIZES = (GLA_QK, GLA_QK, GLA_V, GLA_V, GLA_RANK, SSD_INNER, SSD_CONV_DIM, SSD_HEADS, D_MODEL, D_MODEL)

LANES = 128
SUBLANES = 8

NG = 4
GW = 1280
PW = 256
GA_SPLIT = GW - SSD_INNER
SEG = dict(q=(0, 0), k=(0, GLA_QK), v=(0, 2 * GLA_QK), misc=(0, 2 * GLA_QK + GLA_V),
           g=(1, 0), xbc=(1, GLA_V),
           z=(2, 0), ga0=(2, SSD_INNER),
           ga1=(3, 0), gb=(3, D_MODEL - GA_SPLIT))
MISC_DT = GLA_RANK

_OFF = dict(zip(("q", "k", "v", "g", "a_lr", "z", "xbc", "dt", "ga", "gb"),
                (int(o) for o in np.cumsum((0,) + PROJ_SIZES[:-1]))))
_SRC = dict(q=("main", _OFF["q"], GLA_QK), k=("main", _OFF["k"], GLA_QK), v=("main", _OFF["v"], GLA_V),
            misc=("misc", 0, LANES), g=("main", _OFF["g"], GLA_V), xbc=("main", _OFF["xbc"], SSD_CONV_DIM),
            z=("main", _OFF["z"], SSD_INNER), ga0=("gates", 0, GA_SPLIT), ga1=("gates", GA_SPLIT, D_MODEL - GA_SPLIT),
            gb=("gates", D_MODEL, D_MODEL))
PROJ_PW = 4 * PW
OUT_PW = 2 * PW
_PIECES = [(SEG[n][0], SEG[n][1] + o, min(PROJ_PW, size - o), src, row + o)
           for n, (src, row, size) in _SRC.items() for o in range(0, size, PROJ_PW)]

XH = 2
PROMPT_T = 256
SAMPLE_SB = 16
SSD_PB = 64
VMEM_LIMIT = 56 * 1024 * 1024


def _split3(x):
    hi = x.astype(BF16)
    r1 = x - hi.astype(F32)
    mid = r1.astype(BF16)
    lo = (r1 - mid.astype(F32)).astype(BF16)
    return hi, mid, lo


def _sigmoid(x):
    return 1.0 / (1.0 + jnp.exp(-x))


def _softplus(x):
    return jnp.maximum(x, 0.0) + jnp.log1p(jnp.exp(-jnp.abs(x)))


def _rms_scale(x):
    return lax.rsqrt(jnp.mean(x * x, axis=-1, keepdims=True) + EPS)


def _dot(a, b):
    return jnp.dot(a, b, preferred_element_type=F32)


def _dot_nt(a, b):
    return lax.dot_general(a, b, (((1,), (1,)), ((), ())), preferred_element_type=F32)


def _dot_tn(a, b):
    return lax.dot_general(a, b, (((0,), (0,)), ((), ())), preferred_element_type=F32)


def _consts():
    r = np.arange

    def m(a, dt):
        return jnp.asarray(a.astype(np.float32), dtype=dt)

    c = CHUNK
    return dict(
        tril3=m(r(c)[:, None] >= (r(3 * c)[None, :] % c), BF16),
        km=m(r(GLA_HEADS * c)[:, None] // c == r(GLA_QK)[None, :] // GLA_DK, BF16),
        tril4=m(r(c)[:, None] >= r(GLA_HEADS * c)[None, :] % c, F32),
        vm=m(r(GLA_HEADS * c)[:, None] // c == r(GLA_V)[None, :] // GLA_DV, BF16),
        stm=m(r(GLA_V)[:, None] // GLA_DV == r(GLA_QK)[None, :] // GLA_DK, F32),
        bm=m(r(SSD_HEADS * c)[:, None] // (c * SSD_HEADS // SSD_GROUPS) == r(SSD_BC)[None, :] // SSD_STATE, BF16),
        tril8=m(r(c)[:, None] >= r(SSD_HEADS * c)[None, :] % c, F32),
        diag8=m(r(c)[:, None] == r(SSD_HEADS * c)[None, :] % c, F32),
        xm=m(r(XH * c)[:, None] // c == r(XH * SSD_HEADDIM)[None, :] // SSD_HEADDIM, BF16),
        hm=m(r(SSD_BC)[:, None] // SSD_STATE == r(SSD_INNER)[None, :] // (SSD_INNER // SSD_GROUPS), F32),
        e2=m((r(2 * LANES)[:, None] % LANES) == MISC_DT + r(SSD_INNER)[None, :] // SSD_HEADDIM, BF16),
    )


_CONST_ORDER = ("tril3", "km", "tril4", "vm", "stm", "bm", "tril8", "diag8", "xm", "hm", "e2")


def _pack_params(norm_g, w_in, w_a2, b_a2, gla_norm_g, conv_w, conv_b, dt_bias, a_log, d_skip, ssd_norm_g,
                 w_br_gla, w_br_ssd, w_out, final_norm_g):
    wt = w_in.T
    wm = wt[:_OFF["dt"]].astype(BF16)
    wg = wt[_OFF["ga"]:].astype(BF16)
    wx = jnp.concatenate([wt[_OFF["a_lr"]:_OFF["z"]], wt[_OFF["dt"]:_OFF["ga"]],
                          jnp.zeros((LANES - GLA_RANK - SSD_HEADS, D_MODEL), wt.dtype)], axis=0).astype(BF16)
    wa2 = jnp.concatenate([w_a2, jnp.zeros((LANES - GLA_RANK, GLA_QK), w_a2.dtype)], axis=0).astype(BF16)

    def misc_row(p):
        return jnp.zeros((1, LANES), F32).at[0, MISC_DT:MISC_DT + SSD_HEADS].set(p.astype(F32))

    return dict(
        wm=wm, wg=wg, wx=wx, wa2=wa2,
        wbg=w_br_gla.astype(BF16), wbs=w_br_ssd.astype(BF16), wout=w_out.astype(BF16),
        ng=norm_g.reshape(1, D_MODEL), ba2=b_a2.reshape(1, GLA_QK), glag=gla_norm_g.reshape(1, GLA_V),
        cw=conv_w, cb=conv_b.reshape(1, SSD_CONV_DIM), dtb=misc_row(dt_bias), alog=misc_row(a_log),
        dsk=jnp.repeat(d_skip.astype(F32), SSD_HEADDIM).reshape(1, SSD_INNER),
        ssdg=ssd_norm_g.reshape(1, SSD_INNER), fg=final_norm_g.reshape(1, D_MODEL),
    )


_PARAM_ORDER = ("wm", "wg", "wx", "wa2", "wbg", "wbs", "wout",
                "ng", "ba2", "glag", "cw", "cb", "dtb", "alog", "dsk", "ssdg", "fg")


def _proj_pieces(hn, w_refs, store):
    def piece(gi, l0, width, src, row):
        def emit():
            store(gi, l0, width, _dot_nt(hn(), w_refs[src][row:row + width, :]))
        return emit
    return [piece(*p) for p in _PIECES]


def _a_pad(alog_ref):
    lane = lax.broadcasted_iota(jnp.int32, (1, LANES), 1)
    is_dt = (lane >= MISC_DT) & (lane < MISC_DT + SSD_HEADS)
    return jnp.where(is_dt, -jnp.exp(alog_ref[...]), 0.0)


def _log_sigmoid(x):
    return jnp.minimum(x, 0.0) - jnp.log(1.0 + jnp.exp(-jnp.abs(x)))


def _merge_out(x, o_bf, y_bf, gate_a, gate_b, wbg_ref, wbs_ref, wout_ref, fg_ref):
    merged = _sigmoid(gate_a) * _dot(o_bf, wbg_ref[...]) + _sigmoid(gate_b) * _dot(y_bf, wbs_ref[...])
    out = x + _dot(merged.astype(BF16), wout_ref[...])
    return out * _rms_scale(out) * fg_ref[...]


def _head_rmsnorm(x, width):
    parts = []
    for c0 in range(0, x.shape[-1], width):
        blk = x[:, c0:c0 + width]
        parts.append(blk * _rms_scale(blk))
    return jnp.concatenate(parts, axis=-1)


def _prompt_kernel(T, x_ref, xn_ref, wm_ref, wg_ref, wx_ref, wa2_ref, wbg_ref, wbs_ref, wout_ref,
                   ng_ref, ba2_ref, glag_ref, cw_ref, cb_ref, dtb_ref, alog_ref, dsk_ref, ssdg_ref, fg_ref,
                   tril3_ref, km_ref, tril4_ref, vm_ref, stm_ref, bm_ref, tril8_ref, diag8_ref, xm_ref, hm_ref, e2_ref,
                   y_ref, st_out, hc_out, conv_out,
                   pa_ref, pb_ref, hn_ref, o_buf, y_buf, mg_buf, st_ref, hc_ref):
    t = pl.program_id(1)
    step = pl.program_id(0) * pl.num_programs(1) + t
    c = CHUNK
    tok = slice(SUBLANES, SUBLANES + T)
    last_rows = slice(T, T + SUBLANES)
    gx, lx = SEG["xbc"]
    xbc_l = slice(lx, lx + SSD_CONV_DIM)

    def normalise(src_ref, row0):
        for r0 in range(0, T, c):
            xb = src_ref[row0 + r0:row0 + r0 + c, :]
            hn_ref[r0:r0 + c, :] = (xb * _rms_scale(xb) * ng_ref[...]).astype(BF16)

    w_refs = dict(main=wm_ref, gates=wg_ref, misc=wx_ref)

    def projection(dst_ref):
        def store(gi, l0, width, value):
            dst_ref[gi, tok, l0:l0 + width] = value
        return _proj_pieces(lambda: hn_ref[...], w_refs, store)

    @pl.when(step == 0)
    def _():
        normalise(x_ref, 0)
        for emit in projection(pa_ref):
            emit()

    @pl.when(t == 0)
    def _():
        st_ref[...] = jnp.zeros_like(st_ref)
        hc_ref[...] = jnp.zeros_like(hc_ref)
        pa_ref[gx, 0:SUBLANES, xbc_l] = jnp.zeros((SUBLANES, SSD_CONV_DIM), F32)

    a_pad = _a_pad(alog_ref)

    def run_block(pc_ref, pn_ref, ob, deferred):
      def seg(name, rows, width):
        gi, l0 = SEG[name]
        return pc_ref[gi, rows, l0:l0 + width]


      def stage1(ci):
        r0 = ci * c
        rows = slice(r0 + SUBLANES, r0 + SUBLANES + c)
        misc = seg("misc", rows, LANES)
        pre = _dot(misc.astype(BF16), wa2_ref[...]) + ba2_ref[...]
        dtc = _softplus(misc + dtb_ref[...])
        cum_c = _dot(tril3_ref[...], jnp.concatenate(_split3(dtc * a_pad), axis=0))
        return dict(r0=r0, rows=rows, pre=pre, dtc=dtc, cum_c=cum_c)

      def stage2(d):
        log_a = _log_sigmoid(d["pre"]) * (1.0 / GLA_GATE_TEMP)
        d["b"] = _dot(tril3_ref[...], jnp.concatenate(_split3(log_a), axis=0))
        both = jnp.concatenate([jnp.concatenate(_split3(d["cum_c"])[:2], axis=1),
                                jnp.concatenate(_split3(d["dtc"])[:2], axis=1)], axis=0)
        d["ex"] = _dot(both, e2_ref[...])

      def stage3(d):
        rows, b = d["rows"], d["b"]
        b_last = b[c - 1:c, :]
        k = seg("k", rows, GLA_QK)
        qs = (seg("q", rows, GLA_QK) * (GLA_DK ** -0.5) * jnp.exp(b)).astype(BF16)
        ks = (k * jnp.exp(-b)).astype(BF16)
        kt = jnp.concatenate([ks] * GLA_HEADS, axis=0) * km_ref[...]
        d["a_cat"] = (_dot_nt(qs, kt) * tril4_ref[...]).astype(BF16)
        d["qs"] = qs
        d["kd"] = (k * jnp.exp(b_last - b)).astype(BF16)
        d["eb_last"] = jnp.exp(b_last)

        xe = pc_ref[gx, d["r0"]:d["r0"] + c + SUBLANES, xbc_l]
        x1 = pltpu.roll(xe, 1, 0)
        near = xe[SUBLANES:] * cw_ref[3:4, :] + x1[SUBLANES:] * cw_ref[2:3, :]
        far = pltpu.roll(xe * cw_ref[1:2, :] + x1 * cw_ref[0:1, :], 2, 0)[SUBLANES:]
        xc = near + far + cb_ref[...]
        xc = xc * _sigmoid(xc)
        xs = xc[:, 0:SSD_INNER]
        b_bf = xc[:, SSD_INNER:SSD_INNER + SSD_BC].astype(BF16)
        c_bf = xc[:, SSD_INNER + SSD_BC:].astype(BF16)
        bt = jnp.concatenate([b_bf] * SSD_HEADS, axis=0) * bm_ref[...]
        cum_e = d["ex"][0:c]
        cum_row = jnp.sum(cum_e * diag8_ref[...], axis=0, keepdims=True)
        decay = jnp.exp(jnp.minimum(cum_e - cum_row, 0.0)) * tril8_ref[...]
        d["m_bf"] = (_dot_nt(c_bf, bt) * decay).astype(BF16)
        xdt = xs * d["ex"][c:2 * c]
        c_last = cum_e[c - 1:c, :]
        d["xd"] = (xdt * jnp.exp(c_last - cum_e)).astype(BF16)
        d["xdt_bf"] = xdt.astype(BF16)
        d["ec_last"] = jnp.exp(c_last)
        d["ecum"] = jnp.exp(cum_e)
        d["xs"], d["b_bf"], d["c_bf"] = xs, b_bf, c_bf

      def stage4(d):
        v_bf = seg("v", d["rows"], GLA_V).astype(BF16)
        vbd = jnp.concatenate([v_bf] * GLA_HEADS, axis=0) * vm_ref[...]
        st = st_ref[...]
        d["o"] = _dot(d["a_cat"], vbd) + _dot_nt(d["qs"], st.astype(BF16))
        st_ref[...] = st * d["eb_last"] + _dot_tn(v_bf, d["kd"]) * stm_ref[...]
        ys = []
        for quad in range(SSD_HEADS // XH):
            cols = slice(quad * XH * SSD_HEADDIM, (quad + 1) * XH * SSD_HEADDIM)
            xbd = jnp.concatenate([d["xdt_bf"][:, cols]] * XH, axis=0) * xm_ref[...]
            ys.append(_dot(d["m_bf"][:, cols], xbd))
        hc = hc_ref[...]
        d["y"] = (jnp.concatenate(ys, axis=-1) + _dot(d["c_bf"], hc.astype(BF16)) * d["ecum"]
                  + dsk_ref[...] * d["xs"])
        hc_ref[...] = hc * d["ec_last"] + _dot_tn(d["b_bf"], d["xd"]) * hm_ref[...]

      def stage5(d):
        g = seg("g", d["rows"], GLA_V)
        o = _head_rmsnorm(d["o"], GLA_DV) * glag_ref[...] * (g * _sigmoid(g))
        o_buf[ob, pl.ds(d["r0"], c), :] = o.astype(BF16)
        z = seg("z", d["rows"], SSD_INNER)
        y = _head_rmsnorm(d["y"] * (z * _sigmoid(z)), SSD_INNER // SSD_GROUPS) * ssdg_ref[...]
        y_buf[ob, pl.ds(d["r0"], c), :] = y.astype(BF16)

      fill = projection(pn_ref) + deferred
      nch = T // c
      n_slots = 3 + 2 * nch
      it = iter(fill)

      counts = [0, 1, 1, 1, 1, 1, 1, 1, 1, 1, 1]
      assert sum(counts) == len(fill) and len(counts) == n_slots

      def gap(slot):
        for _ in range(counts[slot]):
            next(it)()

      gap(0)
      ds = [stage1(ci) for ci in range(nch)]
      gap(1)
      for d in ds:
        stage2(d)
      gap(2)
      for k, d in enumerate(ds):
        stage3(d)
        gap(3 + k)
      for k, d in enumerate(ds):
        stage4(d)
        gap(3 + nch + k)
        stage5(d)
      pn_ref[gx, 0:SUBLANES, xbc_l] = pc_ref[gx, last_rows, xbc_l]

    def out_pieces(p_ref, row0, ob):
        prow = slice(SUBLANES, SUBLANES + T)
        xrows = slice(row0, row0 + T)
        nq = D_MODEL // OUT_PW

        def gate_cols(name_lo, name_hi, split, j):
            def tile(c0):
                name, off = (name_lo, c0) if c0 < split else (name_hi, c0 - split)
                gi, l0 = SEG[name]
                return p_ref[gi, prow, l0 + off:l0 + off + PW]
            return jnp.concatenate([tile(c0) for c0 in range(j * OUT_PW, (j + 1) * OUT_PW, PW)], axis=-1)

        def merge(j):
            def emit():
                cols = slice(j * OUT_PW, (j + 1) * OUT_PW)
                ga = gate_cols("ga0", "ga1", GA_SPLIT, j)
                gb = gate_cols("gb", "gb", D_MODEL, j)
                m = (_sigmoid(ga) * _dot(o_buf[ob], wbg_ref[:, cols])
                     + _sigmoid(gb) * _dot(y_buf[ob], wbs_ref[:, cols]))
                mg_buf[:, cols] = m.astype(BF16)
            return emit

        def project(j):
            def emit():
                cols = slice(j * OUT_PW, (j + 1) * OUT_PW)
                y_ref[xrows, cols] = x_ref[xrows, cols] + _dot(mg_buf[...], wout_ref[:, cols])
            return emit

        def final_norm():
            for r0 in range(row0, row0 + T, c):
                v = y_ref[r0:r0 + c, :]
                y_ref[r0:r0 + c, :] = v * _rms_scale(v) * fg_ref[...]

        return [merge(j) for j in range(nq)] + [project(j) for j in range(nq)] + [final_norm]

    normalise(x_ref, T)
    run_block(pa_ref, pb_ref, 0, [])
    for emit in out_pieces(pa_ref, 0, 0):
        emit()
    normalise(xn_ref, 0)
    run_block(pb_ref, pa_ref, 1, [])
    for emit in out_pieces(pb_ref, T, 1):
        emit()

    @pl.when(t == pl.num_programs(1) - 1)
    def _():
        for h in range(GLA_HEADS):
            blk = st_ref[h * GLA_DV:(h + 1) * GLA_DV, :].T
            st_out[h] = blk[h * GLA_DK:(h + 1) * GLA_DK, :]
        hct = hc_ref[...].T
        rep = SSD_HEADS // SSD_GROUPS
        for h in range(SSD_HEADS):
            blk = hct[h * SSD_HEADDIM:(h + 1) * SSD_HEADDIM, :]
            if h // rep:
                blk = pltpu.roll(blk, SSD_BC - (h // rep) * SSD_STATE, 1)
            hc_out[h] = blk[:, 0:SSD_STATE]
        conv_out[...] = pb_ref[gx, last_rows, xbc_l]


def _prompt_call(x, params, consts):
    B, L, _ = x.shape
    T = PROMPT_T
    assert L % (2 * T) == 0 and T == NG * CHUNK and CONV_W == 4 and SSD_GROUPS == 2
    nt = L // (2 * T)
    full = lambda a: pl.BlockSpec(a.shape, lambda b, t: (0,) * a.ndim)
    ins = [params[n] for n in _PARAM_ORDER] + [consts[n] for n in _CONST_ORDER]
    out_shape = (
        jax.ShapeDtypeStruct((B, L, D_MODEL), F32),
        jax.ShapeDtypeStruct((B, GLA_HEADS, GLA_DK, GLA_DV), F32),
        jax.ShapeDtypeStruct((B, SSD_HEADS, SSD_HEADDIM, SSD_STATE), F32),
        jax.ShapeDtypeStruct((B, SUBLANES, SSD_CONV_DIM), F32),
    )

    def next_block(b, t):
        lin = jnp.minimum(b * nt + t + 1, B * nt - 1)
        return (lin // nt, 2 * (lin % nt), 0)

    tok = pl.BlockSpec((None, 2 * T, D_MODEL), lambda b, t: (b, t, 0))
    tok_next = pl.BlockSpec((None, T, D_MODEL), next_block)
    per_b = lambda s: pl.BlockSpec((None,) + s, lambda b, t: (b,) + (0,) * len(s))
    return pl.pallas_call(
        functools.partial(_prompt_kernel, T),
        grid=(B, nt),
        in_specs=[tok, tok_next] + [full(a) for a in ins],
        out_specs=(tok, per_b((GLA_HEADS, GLA_DK, GLA_DV)), per_b((SSD_HEADS, SSD_HEADDIM, SSD_STATE)),
                   per_b((SUBLANES, SSD_CONV_DIM))),
        out_shape=out_shape,
        scratch_shapes=[
            pltpu.VMEM((NG, T + SUBLANES, GW), F32),
            pltpu.VMEM((NG, T + SUBLANES, GW), F32),
            pltpu.VMEM((T, D_MODEL), BF16),
            pltpu.VMEM((2, T, GLA_V), BF16),
            pltpu.VMEM((2, T, SSD_INNER), BF16),
            pltpu.VMEM((T, D_MODEL), BF16),
            pltpu.VMEM((GLA_V, GLA_QK), F32),
            pltpu.VMEM((SSD_BC, SSD_INNER), F32),
        ],
        compiler_params=pltpu.CompilerParams(dimension_semantics=("arbitrary", "arbitrary"),
                                             vmem_limit_bytes=VMEM_LIMIT),
        name="prompt_step",
    )(x, x, *ins)


def _sample_kernel(NB, x_ref, sg_ref, ss_ref, sc_ref, wm_ref, wg_ref, wx_ref, wa2_ref, wbg_ref, wbs_ref, wout_ref,
                   ng_ref, ba2_ref, glag_ref, cw_ref, cb_ref, dtb_ref, alog_ref, dsk_ref, ssdg_ref, fg_ref,
                   y_ref, sg_out, ss_out, sc_out,
                   p_ref, hn_ref, qt_ref, kt_ref, ebt_ref, xdtt_ref, elat_ref, v_ref, xs_ref, bt_ref, ct_ref,
                   o_ref, yt_ref):
    i = pl.program_id(0)
    sb = SAMPLE_SB
    cd = SSD_CONV_DIM
    rep = SSD_HEADS // SSD_GROUPS

    def seg(name, width):
        gi, l0 = SEG[name]
        return p_ref[gi, :, l0:l0 + width]

    @pl.when(i == 0)
    def _():
        xb = x_ref[:, 0, :]
        hn_ref[...] = (xb * _rms_scale(xb) * ng_ref[...]).astype(BF16)
        def store(gi, l0, width, value):
            p_ref[gi, :, l0:l0 + width] = value
        for emit in _proj_pieces(lambda: hn_ref[...], dict(main=wm_ref, gates=wg_ref, misc=wx_ref), store):
            emit()
        misc = seg("misc", LANES)
        pre = _dot(misc.astype(BF16), wa2_ref[...]) + ba2_ref[...]
        log_a = _log_sigmoid(pre) * (1.0 / GLA_GATE_TEMP)
        qt_ref[...] = (seg("q", GLA_QK) * (GLA_DK ** -0.5)).T.astype(BF16)
        kt_ref[...] = seg("k", GLA_QK).T.astype(BF16)
        ebt_ref[...] = jnp.concatenate(_split3(jnp.exp(log_a).T), axis=1)
        v_ref[...] = seg("v", GLA_V)

        xbc = seg("xbc", cd)
        xc = xbc * cw_ref[CONV_W - 1:CONV_W, :] + cb_ref[...]
        for s in range(CONV_W - 1):
            xc = xc + sc_ref[s] * cw_ref[s:s + 1, :]
        for s in range(CONV_W - 2):
            sc_out[s] = sc_ref[s + 1]
        sc_out[CONV_W - 2] = xbc
        xc = xc * _sigmoid(xc)
        xs = xc[:, 0:SSD_INNER]
        xs_ref[...] = xs
        bt_ref[...] = xc[:, SSD_INNER:SSD_INNER + SSD_BC].T
        ct_ref[...] = xc[:, SSD_INNER + SSD_BC:].T
        dtc = _softplus(misc + dtb_ref[...])
        dtt = dtc.T
        elat_ref[...] = jnp.exp(dtc * _a_pad(alog_ref)).T
        xst = xs.T
        for h in range(SSD_HEADS):
            hp = slice(h * SSD_HEADDIM, (h + 1) * SSD_HEADDIM)
            xdtt_ref[hp, :] = xst[hp, :] * dtt[MISC_DT + h:MISC_DT + h + 1, :]

    head = i // (SSD_HEADDIM // SSD_PB)
    grp = pl.multiple_of((head // rep) * SSD_STATE, SSD_STATE)
    ela = elat_ref[pl.ds(MISC_DT + head, 1), :]
    btg = bt_ref[pl.ds(grp, SSD_STATE), :]
    ctg = ct_ref[pl.ds(grp, SSD_STATE), :]
    prow = pl.ds(pl.multiple_of(i * SSD_PB, SSD_PB), SSD_PB)
    xrows = xdtt_ref[prow, :]
    p_iota = lax.broadcasted_iota(jnp.int32, (SSD_PB, 1), 0)
    y_rows = jnp.zeros((SSD_PB, NB), F32)
    for pp in range(SSD_PB):
        h_new = ela * ss_ref[0, pp] + xrows[pp:pp + 1, :] * btg
        ss_out[0, pp] = h_new
        y_rows = jnp.where(p_iota == pp, jnp.sum(ctg * h_new, axis=0, keepdims=True), y_rows)
    yt_ref[prow, :] = y_rows

    blk = pl.ds(pl.multiple_of(i * sb, sb), sb)
    vblk = v_ref[blk, :]
    row_iota = lax.broadcasted_iota(jnp.int32, (NB, LANES), 0)
    sub_iota = lax.broadcasted_iota(jnp.int32, (sb, 1), 0)
    o_blk = jnp.zeros((sb, GLA_V), F32)
    for s in range(sb):
        sel = (row_iota == i * sb + s).astype(BF16)
        sel3 = jnp.concatenate([sel] * 3, axis=0)
        qcol = _dot(qt_ref[...], sel)
        kcol = _dot(kt_ref[...], sel)
        ecol = _dot(ebt_ref[...], sel3)
        o_parts = []
        for h in range(GLA_HEADS):
            hd = slice(h * GLA_DK, (h + 1) * GLA_DK)
            s_new = ecol[hd] * sg_ref[s, h] + kcol[hd] * vblk[s:s + 1, h * GLA_DV:(h + 1) * GLA_DV]
            sg_out[s, h] = s_new
            o_parts.append(jnp.sum(qcol[hd] * s_new, axis=0, keepdims=True))
        o_blk = jnp.where(sub_iota == s, jnp.concatenate(o_parts, axis=-1), o_blk)
    o_ref[blk, :] = o_blk

    @pl.when(i == pl.num_programs(0) - 1)
    def _():
        g = seg("g", GLA_V)
        o = _head_rmsnorm(o_ref[...], GLA_DV) * glag_ref[...] * (g * _sigmoid(g))
        z = seg("z", SSD_INNER)
        y = yt_ref[...].T + dsk_ref[...] * xs_ref[...]
        y = _head_rmsnorm(y * (z * _sigmoid(z)), SSD_INNER // SSD_GROUPS) * ssdg_ref[...]
        gate_a = jnp.concatenate([seg("ga0", GA_SPLIT), seg("ga1", D_MODEL - GA_SPLIT)], axis=-1)
        y_ref[:, 0, :] = _merge_out(x_ref[:, 0, :], o.astype(BF16), y.astype(BF16), gate_a, seg("gb", D_MODEL),
                                wbg_ref, wbs_ref, wout_ref, fg_ref)


def _sample_call(x, state_gla, state_ssm, state_conv, params, consts):
    NB = x.shape[0]
    sb = SAMPLE_SB
    steps = NB // sb
    assert NB == LANES and NB % sb == 0 and steps * SSD_PB == SSD_INNER
    assert x.shape == (NB, 1, D_MODEL)
    ss_t = jnp.transpose(state_ssm, (1, 2, 3, 0))
    sc_t = jnp.transpose(state_conv, (1, 0, 2))
    ins = [params[n] for n in _PARAM_ORDER]
    full = lambda a: pl.BlockSpec(a.shape, lambda i: (0,) * a.ndim)
    sg_spec = pl.BlockSpec((sb, GLA_HEADS, GLA_DK, GLA_DV), lambda i: (i, 0, 0, 0))
    pblocks = SSD_HEADDIM // SSD_PB
    ss_spec = pl.BlockSpec((1, SSD_PB, SSD_STATE, NB), lambda i: (i // pblocks, i % pblocks, 0, 0))
    out_shape = (
        jax.ShapeDtypeStruct(x.shape, F32),
        jax.ShapeDtypeStruct(state_gla.shape, F32),
        jax.ShapeDtypeStruct(ss_t.shape, F32),
        jax.ShapeDtypeStruct(sc_t.shape, F32),
    )
    y, sg, ss, sc = pl.pallas_call(
        functools.partial(_sample_kernel, NB),
        grid=(steps,),
        in_specs=[full(x), sg_spec, ss_spec, full(sc_t)] + [full(a) for a in ins],
        out_specs=(full(x), sg_spec, ss_spec, full(sc_t)),
        out_shape=out_shape,
        scratch_shapes=[
            pltpu.VMEM((NG, NB, GW), F32),
            pltpu.VMEM((NB, D_MODEL), BF16),
            pltpu.VMEM((GLA_QK, NB), BF16),
            pltpu.VMEM((GLA_QK, NB), BF16),
            pltpu.VMEM((GLA_QK, 3 * NB), BF16),
            pltpu.VMEM((SSD_INNER, NB), F32),
            pltpu.VMEM((LANES, NB), F32),
            pltpu.VMEM((NB, GLA_V), F32),
            pltpu.VMEM((NB, SSD_INNER), F32),
            pltpu.VMEM((SSD_BC, NB), F32),
            pltpu.VMEM((SSD_BC, NB), F32),
            pltpu.VMEM((NB, GLA_V), F32),
            pltpu.VMEM((SSD_INNER, NB), F32),
        ],
        compiler_params=pltpu.CompilerParams(dimension_semantics=("arbitrary",), vmem_limit_bytes=VMEM_LIMIT),
        name="decode_step",
    )(x, state_gla, ss_t, sc_t, *ins)
    return y, sg, jnp.transpose(ss, (3, 0, 1, 2)), jnp.transpose(sc, (1, 0, 2))


def kernel(x_prompt, x_sample, state_gla, state_ssm, state_conv, norm_g, w_in, w_a2, b_a2, gla_norm_g, conv_w, conv_b,
           dt_bias, a_log, d_skip, ssd_norm_g, w_br_gla, w_br_ssd, w_out, final_norm_g):
    params = _pack_params(norm_g[0], w_in[0], w_a2[0], b_a2[0], gla_norm_g[0], conv_w[0], conv_b[0], dt_bias[0],
                          a_log[0], d_skip[0], ssd_norm_g[0], w_br_gla[0], w_br_ssd[0], w_out[0], final_norm_g)
    consts = _consts()
    y_p, gla_p, ssm_p, conv_p = _prompt_call(x_prompt, params, consts)
    conv_p = conv_p[:, SUBLANES - (CONV_W - 1):, :]
    y_s, gla_s, ssm_s, conv_s = _sample_call(x_sample, state_gla[0], state_ssm[0], state_conv[0], params, consts)
    return (y_p, y_s, gla_p[None], ssm_p[None], conv_p[None], gla_s[None], ssm_s[None], conv_s[None])
```

```python
import functools

import numpy as np
import jax
import jax.numpy as jnp
from jax import lax
from jax.experimental import pallas as pl
from jax.experimental.pallas import tpu as pltpu

F32 = jnp.float32
BF16 = jnp.bfloat16

D_MODEL = 1024
GLA_HEADS = 4
GLA_DK = 64
GLA_DV = 128
GLA_QK = GLA_HEADS * GLA_DK
GLA_V = GLA_HEADS * GLA_DV
GLA_RANK = 16
GLA_GATE_TEMP = 16.0
SSD_HEADS = 8
SSD_HEADDIM = 64
SSD_INNER = SSD_HEADS * SSD_HEADDIM
SSD_GROUPS = 2
SSD_STATE = 64
SSD_BC = SSD_GROUPS * SSD_STATE
CONV_W = 4
SSD_CONV_DIM = SSD_INNER + 2 * SSD_BC
CHUNK = 64
EPS = 1e-6
PROJ_SIZES = (GLA_QK, GLA_QK, GLA_V, GLA_V, GLA_RANK, SSD_INNER, SSD_CONV_DIM, SSD_HEADS, D_MODEL, D_MODEL)

LANES = 128
SUBLANES = 8

NG = 4
GW = 1280
PW = 256
GA_SPLIT = GW - SSD_INNER
SEG = dict(q=(0, 0), k=(0, GLA_QK), v=(0, 2 * GLA_QK), misc=(0, 2 * GLA_QK + GLA_V),
           g=(1, 0), xbc=(1, GLA_V),
           z=(2, 0), ga0=(2, SSD_INNER),
           ga1=(3, 0), gb=(3, D_MODEL - GA_SPLIT))
MISC_DT = GLA_RANK

_OFF = dict(zip(("q", "k", "v", "g", "a_lr", "z", "xbc", "dt", "ga", "gb"),
                (int(o) for o in np.cumsum((0,) + PROJ_SIZES[:-1]))))
_SRC = dict(q=("main", _OFF["q"], GLA_QK), k=("main", _OFF["k"], GLA_QK), v=("main", _OFF["v"], GLA_V),
            misc=("misc", 0, LANES), g=("main", _OFF["g"], GLA_V), xbc=("main", _OFF["xbc"], SSD_CONV_DIM),
            z=("main", _OFF["z"], SSD_INNER), ga0=("gates", 0, GA_SPLIT), ga1=("gates", GA_SPLIT, D_MODEL - GA_SPLIT),
            gb=("gates", D_MODEL, D_MODEL))
PROJ_PW = 4 * PW
_PIECES = [(SEG[n][0], SEG[n][1] + o, min(PROJ_PW, size - o), src, row + o)
           for n, (src, row, size) in _SRC.items() for o in range(0, size, PROJ_PW)]

XH = 2
PROMPT_T = 256
SAMPLE_SB = 16
SSD_PB = 64
VMEM_LIMIT = 56 * 1024 * 1024


def _split3(x):
    hi = x.astype(BF16)
    r1 = x - hi.astype(F32)
    mid = r1.astype(BF16)
    lo = (r1 - mid.astype(F32)).astype(BF16)
    return hi, mid, lo


def _sigmoid(x):
    return 1.0 / (1.0 + jnp.exp(-x))


def _softplus(x):
    return jnp.maximum(x, 0.0) + jnp.log1p(jnp.exp(-jnp.abs(x)))


def _rms_scale(x):
    return lax.rsqrt(jnp.mean(x * x, axis=-1, keepdims=True) + EPS)


def _dot(a, b):
    return jnp.dot(a, b, preferred_element_type=F32)


def _dot_nt(a, b):
    return lax.dot_general(a, b, (((1,), (1,)), ((), ())), preferred_element_type=F32)


def _dot_tn(a, b):
    return lax.dot_general(a, b, (((0,), (0,)), ((), ())), preferred_element_type=F32)


def _consts():
    r = np.arange

    def m(a, dt):
        return jnp.asarray(a.astype(np.float32), dtype=dt)

    c = CHUNK
    return dict(
        tril3=m(r(c)[:, None] >= (r(3 * c)[None, :] % c), BF16),
        km=m(r(GLA_HEADS * c)[:, None] // c == r(GLA_QK)[None, :] // GLA_DK, BF16),
        tril4=m(r(c)[:, None] >= r(GLA_HEADS * c)[None, :] % c, F32),
        vm=m(r(GLA_HEADS * c)[:, None] // c == r(GLA_V)[None, :] // GLA_DV, BF16),
        stm=m(r(GLA_V)[:, None] // GLA_DV == r(GLA_QK)[None, :] // GLA_DK, F32),
        bm=m(r(SSD_HEADS * c)[:, None] // (c * SSD_HEADS // SSD_GROUPS) == r(SSD_BC)[None, :] // SSD_STATE, BF16),
        tril8=m(r(c)[:, None] >= r(SSD_HEADS * c)[None, :] % c, F32),
        diag8=m(r(c)[:, None] == r(SSD_HEADS * c)[None, :] % c, F32),
        xm=m(r(XH * c)[:, None] // c == r(XH * SSD_HEADDIM)[None, :] // SSD_HEADDIM, BF16),
        hm=m(r(SSD_BC)[:, None] // SSD_STATE == r(SSD_INNER)[None, :] // (SSD_INNER // SSD_GROUPS), F32),
        e2=m((r(2 * LANES)[:, None] % LANES) == MISC_DT + r(SSD_INNER)[None, :] // SSD_HEADDIM, BF16),
    )


_CONST_ORDER = ("tril3", "km", "tril4", "vm", "stm", "bm", "tril8", "diag8", "xm", "hm", "e2")


def _pack_params(norm_g, w_in, w_a2, b_a2, gla_norm_g, conv_w, conv_b, dt_bias, a_log, d_skip, ssd_norm_g,
                 w_br_gla, w_br_ssd, w_out, final_norm_g):
    wt = w_in.T
    wm = wt[:_OFF["dt"]].astype(BF16)
    wg = wt[_OFF["ga"]:].astype(BF16)
    wx = jnp.concatenate([wt[_OFF["a_lr"]:_OFF["z"]], wt[_OFF["dt"]:_OFF["ga"]],
                          jnp.zeros((LANES - GLA_RANK - SSD_HEADS, D_MODEL), wt.dtype)], axis=0).astype(BF16)
    wa2 = jnp.concatenate([w_a2, jnp.zeros((LANES - GLA_RANK, GLA_QK), w_a2.dtype)], axis=0).astype(BF16)

    def misc_row(p):
        return jnp.zeros((1, LANES), F32).at[0, MISC_DT:MISC_DT + SSD_HEADS].set(p.astype(F32))

    return dict(
        wm=wm, wg=wg, wx=wx, wa2=wa2,
        wbg=w_br_gla.astype(BF16), wbs=w_br_ssd.astype(BF16), wout=w_out.astype(BF16),
        ng=norm_g.reshape(1, D_MODEL), ba2=b_a2.reshape(1, GLA_QK), glag=gla_norm_g.reshape(1, GLA_V),
        cw=conv_w, cb=conv_b.reshape(1, SSD_CONV_DIM), dtb=misc_row(dt_bias), alog=misc_row(a_log),
        dsk=jnp.repeat(d_skip.astype(F32), SSD_HEADDIM).reshape(1, SSD_INNER),
        ssdg=ssd_norm_g.reshape(1, SSD_INNER), fg=final_norm_g.reshape(1, D_MODEL),
    )


_PARAM_ORDER = ("wm", "wg", "wx", "wa2", "wbg", "wbs", "wout",
                "ng", "ba2", "glag", "cw", "cb", "dtb", "alog", "dsk", "ssdg", "fg")


def _proj_pieces(hn, w_refs, store):
    def piece(gi, l0, width, src, row):
        def emit():
            store(gi, l0, width, _dot_nt(hn(), w_refs[src][row:row + width, :]))
        return emit
    return [piece(*p) for p in _PIECES]


def _a_pad(alog_ref):
    lane = lax.broadcasted_iota(jnp.int32, (1, LANES), 1)
    is_dt = (lane >= MISC_DT) & (lane < MISC_DT + SSD_HEADS)
    return jnp.where(is_dt, -jnp.exp(alog_ref[...]), 0.0)


def _log_sigmoid(x):
    return jnp.minimum(x, 0.0) - jnp.log(1.0 + jnp.exp(-jnp.abs(x)))


def _merge_out(x, o_bf, y_bf, gate_a, gate_b, wbg_ref, wbs_ref, wout_ref, fg_ref):
    merged = _sigmoid(gate_a) * _dot(o_bf, wbg_ref[...]) + _sigmoid(gate_b) * _dot(y_bf, wbs_ref[...])
    out = x + _dot(merged.astype(BF16), wout_ref[...])
    return out * _rms_scale(out) * fg_ref[...]


def _head_rmsnorm(x, width):
    parts = []
    for c0 in range(0, x.shape[-1], width):
        blk = x[:, c0:c0 + width]
        parts.append(blk * _rms_scale(blk))
    return jnp.concatenate(parts, axis=-1)


def _prompt_kernel(T, x_ref, xn_ref, wm_ref, wg_ref, wx_ref, wa2_ref, wbg_ref, wbs_ref, wout_ref,
                   ng_ref, ba2_ref, glag_ref, cw_ref, cb_ref, dtb_ref, alog_ref, dsk_ref, ssdg_ref, fg_ref,
                   tril3_ref, km_ref, tril4_ref, vm_ref, stm_ref, bm_ref, tril8_ref, diag8_ref, xm_ref, hm_ref, e2_ref,
                   y_ref, st_out, hc_out, conv_out,
                   pa_ref, pb_ref, hn_ref, o_buf, y_buf, mg_buf, st_ref, hc_ref):
    t = pl.program_id(1)
    step = pl.program_id(0) * pl.num_programs(1) + t
    c = CHUNK
    tok = slice(SUBLANES, SUBLANES + T)
    last_rows = slice(T, T + SUBLANES)
    gx, lx = SEG["xbc"]
    xbc_l = slice(lx, lx + SSD_CONV_DIM)

    def normalise(src_ref, row0):
        for r0 in range(0, T, c):
            xb = src_ref[row0 + r0:row0 + r0 + c, :]
            hn_ref[r0:r0 + c, :] = (xb * _rms_scale(xb) * ng_ref[...]).astype(BF16)

    w_refs = dict(main=wm_ref, gates=wg_ref, misc=wx_ref)

    def projection(dst_ref):
        def store(gi, l0, width, value):
            dst_ref[gi, tok, l0:l0 + width] = value
        return _proj_pieces(lambda: hn_ref[...], w_refs, store)

    @pl.when(step == 0)
    def _():
        normalise(x_ref, 0)
        for emit in projection(pa_ref):
            emit()

    @pl.when(t == 0)
    def _():
        st_ref[...] = jnp.zeros_like(st_ref)
        hc_ref[...] = jnp.zeros_like(hc_ref)
        pa_ref[gx, 0:SUBLANES, xbc_l] = jnp.zeros((SUBLANES, SSD_CONV_DIM), F32)

    a_pad = _a_pad(alog_ref)

    def run_block(pc_ref, pn_ref, ob, deferred):
      def seg(name, rows, width):
        gi, l0 = SEG[name]
        return pc_ref[gi, rows, l0:l0 + width]


      def stage1(ci):
        r0 = ci * c
        rows = slice(r0 + SUBLANES, r0 + SUBLANES + c)
        misc = seg("misc", rows, LANES)
        pre = _dot(misc.astype(BF16), wa2_ref[...]) + ba2_ref[...]
        dtc = _softplus(misc + dtb_ref[...])
        cum_c = _dot(tril3_ref[:, 0:2 * c], jnp.concatenate(_split3(dtc * a_pad)[:2], axis=0))
        return dict(r0=r0, rows=rows, pre=pre, dtc=dtc, cum_c=cum_c)

      def stage2(d):
        log_a = _log_sigmoid(d["pre"]) * (1.0 / GLA_GATE_TEMP)
        d["b"] = _dot(tril3_ref[:, 0:2 * c], jnp.concatenate(_split3(log_a)[:2], axis=0))
        both = jnp.concatenate([jnp.concatenate(_split3(d["cum_c"])[:2], axis=1),
                                jnp.concatenate(_split3(d["dtc"])[:2], axis=1)], axis=0)
        d["ex"] = _dot(both, e2_ref[...])

      def stage3(d):
        rows, b = d["rows"], d["b"]
        b_last = b[c - 1:c, :]
        k = seg("k", rows, GLA_QK)
        qs = (seg("q", rows, GLA_QK) * (GLA_DK ** -0.5) * jnp.exp(b)).astype(BF16)
        ks = (k * jnp.exp(-b)).astype(BF16)
        kt = jnp.concatenate([ks] * GLA_HEADS, axis=0) * km_ref[...]
        d["a_cat"] = (_dot_nt(qs, kt) * tril4_ref[...]).astype(BF16)
        d["qs"] = qs
        d["kd"] = (k * jnp.exp(b_last - b)).astype(BF16)
        d["eb_last"] = jnp.exp(b_last)

        xe = pc_ref[gx, d["r0"]:d["r0"] + c + SUBLANES, xbc_l]
        x1 = pltpu.roll(xe, 1, 0)
        near = xe[SUBLANES:] * cw_ref[3:4, :] + x1[SUBLANES:] * cw_ref[2:3, :]
        far = pltpu.roll(xe * cw_ref[1:2, :] + x1 * cw_ref[0:1, :], 2, 0)[SUBLANES:]
        xc = near + far + cb_ref[...]
        xc = xc * _sigmoid(xc)
        xs = xc[:, 0:SSD_INNER]
        b_bf = xc[:, SSD_INNER:SSD_INNER + SSD_BC].astype(BF16)
        c_bf = xc[:, SSD_INNER + SSD_BC:].astype(BF16)
        bt = jnp.concatenate([b_bf] * SSD_HEADS, axis=0) * bm_ref[...]
        cum_e = d["ex"][0:c]
        cum_row = jnp.sum(cum_e * diag8_ref[...], axis=0, keepdims=True)
        decay = jnp.exp(jnp.minimum(cum_e - cum_row, 0.0)) * tril8_ref[...]
        d["m_bf"] = (_dot_nt(c_bf, bt) * decay).astype(BF16)
        xdt = xs * d["ex"][c:2 * c]
        c_last = cum_e[c - 1:c, :]
        d["xd"] = (xdt * jnp.exp(c_last - cum_e)).astype(BF16)
        d["xdt_bf"] = xdt.astype(BF16)
        d["ec_last"] = jnp.exp(c_last)
        d["ecum"] = jnp.exp(cum_e)
        d["xs"], d["b_bf"], d["c_bf"] = xs, b_bf, c_bf

      def stage4(d):
        v_bf = seg("v", d["rows"], GLA_V).astype(BF16)
        vbd = jnp.concatenate([v_bf] * GLA_HEADS, axis=0) * vm_ref[...]
        st = st_ref[...]
        d["o"] = _dot(d["a_cat"], vbd) + _dot_nt(d["qs"], st.astype(BF16))
        st_ref[...] = st * d["eb_last"] + _dot_tn(v_bf, d["kd"]) * stm_ref[...]
        ys = []
        for quad in range(SSD_HEADS // XH):
            cols = slice(quad * XH * SSD_HEADDIM, (quad + 1) * XH * SSD_HEADDIM)
            xbd = jnp.concatenate([d["xdt_bf"][:, cols]] * XH, axis=0) * xm_ref[...]
            ys.append(_dot(d["m_bf"][:, cols], xbd))
        hc = hc_ref[...]
        d["y"] = (jnp.concatenate(ys, axis=-1) + _dot(d["c_bf"], hc.astype(BF16)) * d["ecum"]
                  + dsk_ref[...] * d["xs"])
        hc_ref[...] = hc * d["ec_last"] + _dot_tn(d["b_bf"], d["xd"]) * hm_ref[...]

      def stage5(d):
        g = seg("g", d["rows"], GLA_V)
        o = _head_rmsnorm(d["o"], GLA_DV) * glag_ref[...] * (g * _sigmoid(g))
        o_buf[ob, pl.ds(d["r0"], c), :] = o.astype(BF16)
        z = seg("z", d["rows"], SSD_INNER)
        y = _head_rmsnorm(d["y"] * (z * _sigmoid(z)), SSD_INNER // SSD_GROUPS) * ssdg_ref[...]
        y_buf[ob, pl.ds(d["r0"], c), :] = y.astype(BF16)

      fill = projection(pn_ref) + deferred
      nch = T // c
      n_slots = 3 + 2 * nch
      it = iter(fill)

      counts = [0, 1, 1, 1, 1, 1, 1, 1, 1, 1, 1]
      assert sum(counts) == len(fill) and len(counts) == n_slots

      def gap(slot):
        for _ in range(counts[slot]):
            next(it)()

      gap(0)
      ds = [stage1(ci) for ci in range(nch)]
      gap(1)
      for d in ds:
        stage2(d)
      gap(2)
      for k, d in enumerate(ds):
        stage3(d)
        gap(3 + k)
      for k, d in enumerate(ds):
        stage4(d)
        gap(3 + nch + k)
        stage5(d)
      pn_ref[gx, 0:SUBLANES, xbc_l] = pc_ref[gx, last_rows, xbc_l]

    def out_pieces(p_ref, row0, ob):
        prow = slice(SUBLANES, SUBLANES + T)
        xrows = slice(row0, row0 + T)
        nq = D_MODEL // PW

        def gate_cols(name_lo, name_hi, split, j):
            c0 = j * PW
            name, off = (name_lo, c0) if c0 < split else (name_hi, c0 - split)
            gi, l0 = SEG[name]
            return p_ref[gi, prow, l0 + off:l0 + off + PW]

        def merge(j):
            def emit():
                cols = slice(j * PW, (j + 1) * PW)
                ga = gate_cols("ga0", "ga1", GA_SPLIT, j)
                gb = gate_cols("gb", "gb", D_MODEL, j)
                m = (_sigmoid(ga) * _dot(o_buf[ob], wbg_ref[:, cols])
                     + _sigmoid(gb) * _dot(y_buf[ob], wbs_ref[:, cols]))
                mg_buf[:, cols] = m.astype(BF16)
            return emit

        def project(j):
            def emit():
                cols = slice(j * PW, (j + 1) * PW)
                y_ref[xrows, cols] = x_ref[xrows, cols] + _dot(mg_buf[...], wout_ref[:, cols])
            return emit

        def final_norm():
            for r0 in range(row0, row0 + T, c):
                v = y_ref[r0:r0 + c, :]
                y_ref[r0:r0 + c, :] = v * _rms_scale(v) * fg_ref[...]

        return [merge(j) for j in range(nq)] + [project(j) for j in range(nq)] + [final_norm]

    normalise(x_ref, T)
    run_block(pa_ref, pb_ref, 0, [])
    for emit in out_pieces(pa_ref, 0, 0):
        emit()
    normalise(xn_ref, 0)
    run_block(pb_ref, pa_ref, 1, [])
    for emit in out_pieces(pb_ref, T, 1):
        emit()

    @pl.when(t == pl.num_programs(1) - 1)
    def _():
        for h in range(GLA_HEADS):
            blk = st_ref[h * GLA_DV:(h + 1) * GLA_DV, :].T
            st_out[h] = blk[h * GLA_DK:(h + 1) * GLA_DK, :]
        hct = hc_ref[...].T
        rep = SSD_HEADS // SSD_GROUPS
        for h in range(SSD_HEADS):
            blk = hct[h * SSD_HEADDIM:(h + 1) * SSD_HEADDIM, :]
            if h // rep:
                blk = pltpu.roll(blk, SSD_BC - (h // rep) * SSD_STATE, 1)
            hc_out[h] = blk[:, 0:SSD_STATE]
        conv_out[...] = pb_ref[gx, last_rows, xbc_l]


def _prompt_call(x, params, consts):
    B, L, _ = x.shape
    T = PROMPT_T
    assert L % (2 * T) == 0 and T == NG * CHUNK and CONV_W == 4 and SSD_GROUPS == 2
    nt = L // (2 * T)
    full = lambda a: pl.BlockSpec(a.shape, lambda b, t: (0,) * a.ndim)
    ins = [params[n] for n in _PARAM_ORDER] + [consts[n] for n in _CONST_ORDER]
    out_shape = (
        jax.ShapeDtypeStruct((B, L, D_MODEL), F32),
        jax.ShapeDtypeStruct((B, GLA_HEADS, GLA_DK, GLA_DV), F32),
        jax.ShapeDtypeStruct((B, SSD_HEADS, SSD_HEADDIM, SSD_STATE), F32),
        jax.ShapeDtypeStruct((B, SUBLANES, SSD_CONV_DIM), F32),
    )

    def next_block(b, t):
        lin = jnp.minimum(b * nt + t + 1, B * nt - 1)
        return (lin // nt, 2 * (lin % nt), 0)

    tok = pl.BlockSpec((None, 2 * T, D_MODEL), lambda b, t: (b, t, 0))
    tok_next = pl.BlockSpec((None, T, D_MODEL), next_block)
    per_b = lambda s: pl.BlockSpec((None,) + s, lambda b, t: (b,) + (0,) * len(s))
    return pl.pallas_call(
        functools.partial(_prompt_kernel, T),
        grid=(B, nt),
        in_specs=[tok, tok_next] + [full(a) for a in ins],
        out_specs=(tok, per_b((GLA_HEADS, GLA_DK, GLA_DV)), per_b((SSD_HEADS, SSD_HEADDIM, SSD_STATE)),
                   per_b((SUBLANES, SSD_CONV_DIM))),
        out_shape=out_shape,
        scratch_shapes=[
            pltpu.VMEM((NG, T + SUBLANES, GW), F32),
            pltpu.VMEM((NG, T + SUBLANES, GW), F32),
            pltpu.VMEM((T, D_MODEL), BF16),
            pltpu.VMEM((2, T, GLA_V), BF16),
            pltpu.VMEM((2, T, SSD_INNER), BF16),
            pltpu.VMEM((T, D_MODEL), BF16),
            pltpu.VMEM((GLA_V, GLA_QK), F32),
            pltpu.VMEM((SSD_BC, SSD_INNER), F32),
        ],
        compiler_params=pltpu.CompilerParams(dimension_semantics=("arbitrary", "arbitrary"),
                                             vmem_limit_bytes=VMEM_LIMIT),
        name="prompt_step",
    )(x, x, *ins)


def _sample_kernel(NB, x_ref, sg_ref, ss_ref, sc_ref, wm_ref, wg_ref, wx_ref, wa2_ref, wbg_ref, wbs_ref, wout_ref,
                   ng_ref, ba2_ref, glag_ref, cw_ref, cb_ref, dtb_ref, alog_ref, dsk_ref, ssdg_ref, fg_ref,
                   y_ref, sg_out, ss_out, sc_out,
                   p_ref, hn_ref, qt_ref, kt_ref, ebt_ref, xdtt_ref, elat_ref, v_ref, xs_ref, bt_ref, ct_ref,
                   o_ref, yt_ref):
    i = pl.program_id(0)
    sb = SAMPLE_SB
    cd = SSD_CONV_DIM
    rep = SSD_HEADS // SSD_GROUPS

    def seg(name, width):
        gi, l0 = SEG[name]
        return p_ref[gi, :, l0:l0 + width]

    @pl.when(i == 0)
    def _():
        xb = x_ref[:, 0, :]
        hn_ref[...] = (xb * _rms_scale(xb) * ng_ref[...]).astype(BF16)
        def store(gi, l0, width, value):
            p_ref[gi, :, l0:l0 + width] = value
        for emit in _proj_pieces(lambda: hn_ref[...], dict(main=wm_ref, gates=wg_ref, misc=wx_ref), store):
            emit()
        misc = seg("misc", LANES)
        pre = _dot(misc.astype(BF16), wa2_ref[...]) + ba2_ref[...]
        log_a = _log_sigmoid(pre) * (1.0 / GLA_GATE_TEMP)
        qt_ref[...] = (seg("q", GLA_QK) * (GLA_DK ** -0.5)).T.astype(BF16)
        kt_ref[...] = seg("k", GLA_QK).T.astype(BF16)
        ebt_ref[...] = jnp.concatenate(_split3(jnp.exp(log_a).T), axis=1)
        v_ref[...] = seg("v", GLA_V)

        xbc = seg("xbc", cd)
        xc = xbc * cw_ref[CONV_W - 1:CONV_W, :] + cb_ref[...]
        for s in range(CONV_W - 1):
            xc = xc + sc_ref[s] * cw_ref[s:s + 1, :]
        for s in range(CONV_W - 2):
            sc_out[s] = sc_ref[s + 1]
        sc_out[CONV_W - 2] = xbc
        xc = xc * _sigmoid(xc)
        xs = xc[:, 0:SSD_INNER]
        xs_ref[...] = xs
        bt_ref[...] = xc[:, SSD_INNER:SSD_INNER + SSD_BC].T
        ct_ref[...] = xc[:, SSD_INNER + SSD_BC:].T
        dtc = _softplus(misc + dtb_ref[...])
        dtt = dtc.T
        elat_ref[...] = jnp.exp(dtc * _a_pad(alog_ref)).T
        xst = xs.T
        for h in range(SSD_HEADS):
            hp = slice(h * SSD_HEADDIM, (h + 1) * SSD_HEADDIM)
            xdtt_ref[hp, :] = xst[hp, :] * dtt[MISC_DT + h:MISC_DT + h + 1, :]

    head = i // (SSD_HEADDIM // SSD_PB)
    grp = pl.multiple_of((head // rep) * SSD_STATE, SSD_STATE)
    ela = elat_ref[pl.ds(MISC_DT + head, 1), :]
    btg = bt_ref[pl.ds(grp, SSD_STATE), :]
    ctg = ct_ref[pl.ds(grp, SSD_STATE), :]
    prow = pl.ds(pl.multiple_of(i * SSD_PB, SSD_PB), SSD_PB)
    xrows = xdtt_ref[prow, :]
    p_iota = lax.broadcasted_iota(jnp.int32, (SSD_PB, 1), 0)
    y_rows = jnp.zeros((SSD_PB, NB), F32)
    for pp in range(SSD_PB):
        h_new = ela * ss_ref[0, pp] + xrows[pp:pp + 1, :] * btg
        ss_out[0, pp] = h_new
        y_rows = jnp.where(p_iota == pp, jnp.sum(ctg * h_new, axis=0, keepdims=True), y_rows)
    yt_ref[prow, :] = y_rows

    blk = pl.ds(pl.multiple_of(i * sb, sb), sb)
    vblk = v_ref[blk, :]
    row_iota = lax.broadcasted_iota(jnp.int32, (NB, LANES), 0)
    sub_iota = lax.broadcasted_iota(jnp.int32, (sb, 1), 0)
    o_blk = jnp.zeros((sb, GLA_V), F32)
    for s in range(sb):
        sel = (row_iota == i * sb + s).astype(BF16)
        sel3 = jnp.concatenate([sel] * 3, axis=0)
        qcol = _dot(qt_ref[...], sel)
        kcol = _dot(kt_ref[...], sel)
        ecol = _dot(ebt_ref[...], sel3)
        o_parts = []
        for h in range(GLA_HEADS):
            hd = slice(h * GLA_DK, (h + 1) * GLA_DK)
            s_new = ecol[hd] * sg_ref[s, h] + kcol[hd] * vblk[s:s + 1, h * GLA_DV:(h + 1) * GLA_DV]
            sg_out[s, h] = s_new
            o_parts.append(jnp.sum(qcol[hd] * s_new, axis=0, keepdims=True))
        o_blk = jnp.where(sub_iota == s, jnp.concatenate(o_parts, axis=-1), o_blk)
    o_ref[blk, :] = o_blk

    @pl.when(i == pl.num_programs(0) - 1)
    def _():
        g = seg("g", GLA_V)
        o = _head_rmsnorm(o_ref[...], GLA_DV) * glag_ref[...] * (g * _sigmoid(g))
        z = seg("z", SSD_INNER)
        y = yt_ref[...].T + dsk_ref[...] * xs_ref[...]
        y = _head_rmsnorm(y * (z * _sigmoid(z)), SSD_INNER // SSD_GROUPS) * ssdg_ref[...]
        gate_a = jnp.concatenate([seg("ga0", GA_SPLIT), seg("ga1", D_MODEL - GA_SPLIT)], axis=-1)
        y_ref[:, 0, :] = _merge_out(x_ref[:, 0, :], o.astype(BF16), y.astype(BF16), gate_a, seg("gb", D_MODEL),
                                wbg_ref, wbs_ref, wout_ref, fg_ref)


def _sample_call(x, state_gla, state_ssm, state_conv, params, consts):
    NB = x.shape[0]
    sb = SAMPLE_SB
    steps = NB // sb
    assert NB == LANES and NB % sb == 0 and steps * SSD_PB == SSD_INNER
    assert x.shape == (NB, 1, D_MODEL)
    ss_t = jnp.transpose(state_ssm, (1, 2, 3, 0))
    sc_t = jnp.transpose(state_conv, (1, 0, 2))
    ins = [params[n] for n in _PARAM_ORDER]
    full = lambda a: pl.BlockSpec(a.shape, lambda i: (0,) * a.ndim)
    sg_spec = pl.BlockSpec((sb, GLA_HEADS, GLA_DK, GLA_DV), lambda i: (i, 0, 0, 0))
    pblocks = SSD_HEADDIM // SSD_PB
    ss_spec = pl.BlockSpec((1, SSD_PB, SSD_STATE, NB), lambda i: (i // pblocks, i % pblocks, 0, 0))
    out_shape = (
        jax.ShapeDtypeStruct(x.shape, F32),
        jax.ShapeDtypeStruct(state_gla.shape, F32),
        jax.ShapeDtypeStruct(ss_t.shape, F32),
        jax.ShapeDtypeStruct(sc_t.shape, F32),
    )
    y, sg, ss, sc = pl.pallas_call(
        functools.partial(_sample_kernel, NB),
        grid=(steps,),
        in_specs=[full(x), sg_spec, ss_spec, full(sc_t)] + [full(a) for a in ins],
        out_specs=(full(x), sg_spec, ss_spec, full(sc_t)),
        out_shape=out_shape,
        scratch_shapes=[
            pltpu.VMEM((NG, NB, GW), F32),
            pltpu.VMEM((NB, D_MODEL), BF16),
            pltpu.VMEM((GLA_QK, NB), BF16),
            pltpu.VMEM((GLA_QK, NB), BF16),
            pltpu.VMEM((GLA_QK, 3 * NB), BF16),
            pltpu.VMEM((SSD_INNER, NB), F32),
            pltpu.VMEM((LANES, NB), F32),
            pltpu.VMEM((NB, GLA_V), F32),
            pltpu.VMEM((NB, SSD_INNER), F32),
            pltpu.VMEM((SSD_BC, NB), F32),
            pltpu.VMEM((SSD_BC, NB), F32),
            pltpu.VMEM((NB, GLA_V), F32),
            pltpu.VMEM((SSD_INNER, NB), F32),
        ],
        compiler_params=pltpu.CompilerParams(dimension_semantics=("arbitrary",), vmem_limit_bytes=VMEM_LIMIT),
        name="decode_step",
    )(x, state_gla, ss_t, sc_t, *ins)
    return y, sg, jnp.transpose(ss, (3, 0, 1, 2)), jnp.transpose(sc, (1, 0, 2))


def kernel(x_prompt, x_sample, state_gla, state_ssm, state_conv, norm_g, w_in, w_a2, b_a2, gla_norm_g, conv_w, conv_b,
           dt_bias, a_log, d_skip, ssd_norm_g, w_br_gla, w_br_ssd, w_out, final_norm_g):
    params = _pack_params(norm_g[0], w_in[0], w_a2[0], b_a2[0], gla_norm_g[0], conv_w[0], conv_b[0], dt_bias[0],
                          a_log[0], d_skip[0], ssd_norm_g[0], w_br_gla[0], w_br_ssd[0], w_out[0], final_norm_g)
    consts = _consts()
    y_p, gla_p, ssm_p, conv_p = _prompt_call(x_prompt, params, consts)
    conv_p = conv_p[:, SUBLANES - (CONV_W - 1):, :]
    y_s, gla_s, ssm_s, conv_s = _sample_call(x_sample, state_gla[0], state_ssm[0], state_conv[0], params, consts)
    return (y_p, y_s, gla_p[None], ssm_p[None], conv_p[None], gla_s[None], ssm_s[None], conv_s[None])
```

```python
import functools

import numpy as np
import jax
import jax.numpy as jnp
from jax import lax
from jax.experimental import pallas as pl
from jax.experimental.pallas import tpu as pltpu

F32 = jnp.float32
BF16 = jnp.bfloat16

D_MODEL = 1024
GLA_HEADS = 4
GLA_DK = 64
GLA_DV = 128
GLA_QK = GLA_HEADS * GLA_DK
GLA_V = GLA_HEADS * GLA_DV
GLA_RANK = 16
GLA_GATE_TEMP = 16.0
SSD_HEADS = 8
SSD_HEADDIM = 64
SSD_INNER = SSD_HEADS * SSD_HEADDIM
SSD_GROUPS = 2
SSD_STATE = 64
SSD_BC = SSD_GROUPS * SSD_STATE
CONV_W = 4
SSD_CONV_DIM = SSD_INNER + 2 * SSD_BC
CHUNK = 64
EPS = 1e-6
PROJ_SIZES = (GLA_QK, GLA_QK, GLA_V, GLA_V, GLA_RANK, SSD_INNER, SSD_CONV_DIM, SSD_HEADS, D_MODEL, D_MODEL)

LANES = 128
SUBLANES = 8

NG = 4
GW = 1280
PW = 256
GA_SPLIT = GW - SSD_INNER
SEG = dict(q=(0, 0), k=(0, GLA_QK), v=(0, 2 * GLA_QK), misc=(0, 2 * GLA_QK + GLA_V),
           g=(1, 0), xbc=(1, GLA_V),
           z=(2, 0), ga0=(2, SSD_INNER),
           ga1=(3, 0), gb=(3, D_MODEL - GA_SPLIT))
MISC_DT = GLA_RANK

_OFF = dict(zip(("q", "k", "v", "g", "a_lr", "z", "xbc", "dt", "ga", "gb"),
                (int(o) for o in np.cumsum((0,) + PROJ_SIZES[:-1]))))
_SRC = dict(q=("main", _OFF["q"], GLA_QK), k=("main", _OFF["k"], GLA_QK), v=("main", _OFF["v"], GLA_V),
            misc=("misc", 0, LANES), g=("main", _OFF["g"], GLA_V), xbc=("main", _OFF["xbc"], SSD_CONV_DIM),
            z=("main", _OFF["z"], SSD_INNER), ga0=("gates", 0, GA_SPLIT), ga1=("gates", GA_SPLIT, D_MODEL - GA_SPLIT),
            gb=("gates", D_MODEL, D_MODEL))
PROJ_PW = 4 * PW
_PIECES = [(SEG[n][0], SEG[n][1] + o, min(PROJ_PW, size - o), src, row + o)
           for n, (src, row, size) in _SRC.items() for o in range(0, size, PROJ_PW)]

XH = 2
PROMPT_T = 256
STEP_BLOCKS = 4
SAMPLE_SB = 16
SSD_PB = 64
VMEM_LIMIT = 56 * 1024 * 1024


def _split3(x):
    hi = x.astype(BF16)
    r1 = x - hi.astype(F32)
    mid = r1.astype(BF16)
    lo = (r1 - mid.astype(F32)).astype(BF16)
    return hi, mid, lo


def _sigmoid(x):
    return 1.0 / (1.0 + jnp.exp(-x))


def _softplus(x):
    return jnp.maximum(x, 0.0) + jnp.log1p(jnp.exp(-jnp.abs(x)))


def _rms_scale(x):
    return lax.rsqrt(jnp.mean(x * x, axis=-1, keepdims=True) + EPS)


def _dot(a, b):
    return jnp.dot(a, b, preferred_element_type=F32)


def _dot_nt(a, b):
    return lax.dot_general(a, b, (((1,), (1,)), ((), ())), preferred_element_type=F32)


def _dot_tn(a, b):
    return lax.dot_general(a, b, (((0,), (0,)), ((), ())), preferred_element_type=F32)


def _consts():
    r = np.arange

    def m(a, dt):
        return jnp.asarray(a.astype(np.float32), dtype=dt)

    c = CHUNK
    return dict(
        tril3=m(r(c)[:, None] >= (r(3 * c)[None, :] % c), BF16),
        km=m(r(GLA_HEADS * c)[:, None] // c == r(GLA_QK)[None, :] // GLA_DK, BF16),
        tril4=m(r(c)[:, None] >= r(GLA_HEADS * c)[None, :] % c, F32),
        vm=m(r(GLA_HEADS * c)[:, None] // c == r(GLA_V)[None, :] // GLA_DV, BF16),
        stm=m(r(GLA_V)[:, None] // GLA_DV == r(GLA_QK)[None, :] // GLA_DK, F32),
        bm=m(r(SSD_HEADS * c)[:, None] // (c * SSD_HEADS // SSD_GROUPS) == r(SSD_BC)[None, :] // SSD_STATE, BF16),
        tril8=m(r(c)[:, None] >= r(SSD_HEADS * c)[None, :] % c, F32),
        diag8=m(r(c)[:, None] == r(SSD_HEADS * c)[None, :] % c, F32),
        xm=m(r(XH * c)[:, None] // c == r(XH * SSD_HEADDIM)[None, :] // SSD_HEADDIM, BF16),
        hm=m(r(SSD_BC)[:, None] // SSD_STATE == r(SSD_INNER)[None, :] // (SSD_INNER // SSD_GROUPS), F32),
        e2=m((r(2 * LANES)[:, None] % LANES) == MISC_DT + r(SSD_INNER)[None, :] // SSD_HEADDIM, BF16),
    )


_CONST_ORDER = ("tril3", "km", "tril4", "vm", "stm", "bm", "tril8", "diag8", "xm", "hm", "e2")


def _pack_params(norm_g, w_in, w_a2, b_a2, gla_norm_g, conv_w, conv_b, dt_bias, a_log, d_skip, ssd_norm_g,
                 w_br_gla, w_br_ssd, w_out, final_norm_g):
    wt = w_in.T
    wm = wt[:_OFF["dt"]].astype(BF16)
    wg = wt[_OFF["ga"]:].astype(BF16)
    wx = jnp.concatenate([wt[_OFF["a_lr"]:_OFF["z"]], wt[_OFF["dt"]:_OFF["ga"]],
                          jnp.zeros((LANES - GLA_RANK - SSD_HEADS, D_MODEL), wt.dtype)], axis=0).astype(BF16)
    wa2 = jnp.concatenate([w_a2, jnp.zeros((LANES - GLA_RANK, GLA_QK), w_a2.dtype)], axis=0).astype(BF16)

    def misc_row(p):
        return jnp.zeros((1, LANES), F32).at[0, MISC_DT:MISC_DT + SSD_HEADS].set(p.astype(F32))

    return dict(
        wm=wm, wg=wg, wx=wx, wa2=wa2,
        wbg=w_br_gla.astype(BF16), wbs=w_br_ssd.astype(BF16), wout=w_out.astype(BF16),
        ng=norm_g.reshape(1, D_MODEL), ba2=b_a2.reshape(1, GLA_QK), glag=gla_norm_g.reshape(1, GLA_V),
        cw=conv_w, cb=conv_b.reshape(1, SSD_CONV_DIM), dtb=misc_row(dt_bias), alog=misc_row(a_log),
        dsk=jnp.repeat(d_skip.astype(F32), SSD_HEADDIM).reshape(1, SSD_INNER),
        ssdg=ssd_norm_g.reshape(1, SSD_INNER), fg=final_norm_g.reshape(1, D_MODEL),
    )


_PARAM_ORDER = ("wm", "wg", "wx", "wa2", "wbg", "wbs", "wout",
                "ng", "ba2", "glag", "cw", "cb", "dtb", "alog", "dsk", "ssdg", "fg")


def _proj_pieces(hn, w_refs, store):
    def piece(gi, l0, width, src, row):
        def emit():
            store(gi, l0, width, _dot_nt(hn(), w_refs[src][row:row + width, :]))
        return emit
    return [piece(*p) for p in _PIECES]


def _a_pad(alog_ref):
    lane = lax.broadcasted_iota(jnp.int32, (1, LANES), 1)
    is_dt = (lane >= MISC_DT) & (lane < MISC_DT + SSD_HEADS)
    return jnp.where(is_dt, -jnp.exp(alog_ref[...]), 0.0)


def _log_sigmoid(x):
    return jnp.minimum(x, 0.0) - jnp.log(1.0 + jnp.exp(-jnp.abs(x)))


def _merge_out(x, o_bf, y_bf, gate_a, gate_b, wbg_ref, wbs_ref, wout_ref, fg_ref):
    merged = _sigmoid(gate_a) * _dot(o_bf, wbg_ref[...]) + _sigmoid(gate_b) * _dot(y_bf, wbs_ref[...])
    out = x + _dot(merged.astype(BF16), wout_ref[...])
    return out * _rms_scale(out) * fg_ref[...]


def _head_rmsnorm(x, width):
    parts = []
    for c0 in range(0, x.shape[-1], width):
        blk = x[:, c0:c0 + width]
        parts.append(blk * _rms_scale(blk))
    return jnp.concatenate(parts, axis=-1)


def _prompt_kernel(T, x_ref, xn_ref, wm_ref, wg_ref, wx_ref, wa2_ref, wbg_ref, wbs_ref, wout_ref,
                   ng_ref, ba2_ref, glag_ref, cw_ref, cb_ref, dtb_ref, alog_ref, dsk_ref, ssdg_ref, fg_ref,
                   tril3_ref, km_ref, tril4_ref, vm_ref, stm_ref, bm_ref, tril8_ref, diag8_ref, xm_ref, hm_ref, e2_ref,
                   y_ref, st_out, hc_out, conv_out,
                   pa_ref, pb_ref, hn_ref, o_buf, y_buf, mg_buf, st_ref, hc_ref):
    t = pl.program_id(1)
    step = pl.program_id(0) * pl.num_programs(1) + t
    c = CHUNK
    tok = slice(SUBLANES, SUBLANES + T)
    last_rows = slice(T, T + SUBLANES)
    gx, lx = SEG["xbc"]
    xbc_l = slice(lx, lx + SSD_CONV_DIM)

    def normalise(src_ref, row0):
        for r0 in range(0, T, c):
            xb = src_ref[row0 + r0:row0 + r0 + c, :]
            hn_ref[r0:r0 + c, :] = (xb * _rms_scale(xb) * ng_ref[...]).astype(BF16)

    w_refs = dict(main=wm_ref, gates=wg_ref, misc=wx_ref)

    def projection(dst_ref):
        def store(gi, l0, width, value):
            dst_ref[gi, tok, l0:l0 + width] = value
        return _proj_pieces(lambda: hn_ref[...], w_refs, store)

    @pl.when(step == 0)
    def _():
        normalise(x_ref, 0)
        for emit in projection(pa_ref):
            emit()

    @pl.when(t == 0)
    def _():
        st_ref[...] = jnp.zeros_like(st_ref)
        hc_ref[...] = jnp.zeros_like(hc_ref)
        pa_ref[gx, 0:SUBLANES, xbc_l] = jnp.zeros((SUBLANES, SSD_CONV_DIM), F32)

    a_pad = _a_pad(alog_ref)

    def run_block(pc_ref, pn_ref, ob, deferred):
      def seg(name, rows, width):
        gi, l0 = SEG[name]
        return pc_ref[gi, rows, l0:l0 + width]


      def stage1(ci):
        r0 = ci * c
        rows = slice(r0 + SUBLANES, r0 + SUBLANES + c)
        misc = seg("misc", rows, LANES)
        pre = _dot(misc.astype(BF16), wa2_ref[...]) + ba2_ref[...]
        dtc = _softplus(misc + dtb_ref[...])
        cum_c = _dot(tril3_ref[:, 0:2 * c], jnp.concatenate(_split3(dtc * a_pad)[:2], axis=0))
        return dict(r0=r0, rows=rows, pre=pre, dtc=dtc, cum_c=cum_c)

      def stage2(d):
        log_a = _log_sigmoid(d["pre"]) * (1.0 / GLA_GATE_TEMP)
        d["b"] = _dot(tril3_ref[:, 0:2 * c], jnp.concatenate(_split3(log_a)[:2], axis=0))
        both = jnp.concatenate([jnp.concatenate(_split3(d["cum_c"])[:2], axis=1),
                                jnp.concatenate(_split3(d["dtc"])[:2], axis=1)], axis=0)
        d["ex"] = _dot(both, e2_ref[...])

      def stage3(d):
        rows, b = d["rows"], d["b"]
        b_last = b[c - 1:c, :]
        k = seg("k", rows, GLA_QK)
        qs = (seg("q", rows, GLA_QK) * (GLA_DK ** -0.5) * jnp.exp(b)).astype(BF16)
        ks = (k * jnp.exp(-b)).astype(BF16)
        kt = jnp.concatenate([ks] * GLA_HEADS, axis=0) * km_ref[...]
        d["a_cat"] = (_dot_nt(qs, kt) * tril4_ref[...]).astype(BF16)
        d["qs"] = qs
        d["kd"] = (k * jnp.exp(b_last - b)).astype(BF16)
        d["eb_last"] = jnp.exp(b_last)

        xe = pc_ref[gx, d["r0"]:d["r0"] + c + SUBLANES, xbc_l]
        x1 = pltpu.roll(xe, 1, 0)
        near = xe[SUBLANES:] * cw_ref[3:4, :] + x1[SUBLANES:] * cw_ref[2:3, :]
        far = pltpu.roll(xe * cw_ref[1:2, :] + x1 * cw_ref[0:1, :], 2, 0)[SUBLANES:]
        xc = near + far + cb_ref[...]
        xc = xc * _sigmoid(xc)
        xs = xc[:, 0:SSD_INNER]
        b_bf = xc[:, SSD_INNER:SSD_INNER + SSD_BC].astype(BF16)
        c_bf = xc[:, SSD_INNER + SSD_BC:].astype(BF16)
        bt = jnp.concatenate([b_bf] * SSD_HEADS, axis=0) * bm_ref[...]
        cum_e = d["ex"][0:c]
        cum_row = jnp.sum(cum_e * diag8_ref[...], axis=0, keepdims=True)
        decay = jnp.exp(jnp.minimum(cum_e - cum_row, 0.0)) * tril8_ref[...]
        d["m_bf"] = (_dot_nt(c_bf, bt) * decay).astype(BF16)
        xdt = xs * d["ex"][c:2 * c]
        c_last = cum_e[c - 1:c, :]
        d["xd"] = (xdt * jnp.exp(c_last - cum_e)).astype(BF16)
        d["xdt_bf"] = xdt.astype(BF16)
        d["ec_last"] = jnp.exp(c_last)
        d["ecum"] = jnp.exp(cum_e)
        d["xs"], d["b_bf"], d["c_bf"] = xs, b_bf, c_bf

      def stage4(d):
        v_bf = seg("v", d["rows"], GLA_V).astype(BF16)
        vbd = jnp.concatenate([v_bf] * GLA_HEADS, axis=0) * vm_ref[...]
        st = st_ref[...]
        d["o"] = _dot(d["a_cat"], vbd) + _dot_nt(d["qs"], st.astype(BF16))
        st_ref[...] = st * d["eb_last"] + _dot_tn(v_bf, d["kd"]) * stm_ref[...]
        ys = []
        for quad in range(SSD_HEADS // XH):
            cols = slice(quad * XH * SSD_HEADDIM, (quad + 1) * XH * SSD_HEADDIM)
            xbd = jnp.concatenate([d["xdt_bf"][:, cols]] * XH, axis=0) * xm_ref[...]
            ys.append(_dot(d["m_bf"][:, cols], xbd))
        hc = hc_ref[...]
        d["y"] = (jnp.concatenate(ys, axis=-1) + _dot(d["c_bf"], hc.astype(BF16)) * d["ecum"]
                  + dsk_ref[...] * d["xs"])
        hc_ref[...] = hc * d["ec_last"] + _dot_tn(d["b_bf"], d["xd"]) * hm_ref[...]

      def stage5(d):
        g = seg("g", d["rows"], GLA_V)
        o = _head_rmsnorm(d["o"], GLA_DV) * glag_ref[...] * (g * _sigmoid(g))
        o_buf[ob, pl.ds(d["r0"], c), :] = o.astype(BF16)
        z = seg("z", d["rows"], SSD_INNER)
        y = _head_rmsnorm(d["y"] * (z * _sigmoid(z)), SSD_INNER // SSD_GROUPS) * ssdg_ref[...]
        y_buf[ob, pl.ds(d["r0"], c), :] = y.astype(BF16)

      fill = projection(pn_ref) + deferred
      nch = T // c
      n_slots = 3 + 2 * nch
      it = iter(fill)

      counts = [0, 1, 1, 1, 1, 1, 1, 1, 1, 1, 1]
      assert sum(counts) == len(fill) and len(counts) == n_slots

      def gap(slot):
        for _ in range(counts[slot]):
            next(it)()

      gap(0)
      ds = [stage1(ci) for ci in range(nch)]
      gap(1)
      for d in ds:
        stage2(d)
      gap(2)
      for k, d in enumerate(ds):
        stage3(d)
        gap(3 + k)
      for k, d in enumerate(ds):
        stage4(d)
        gap(3 + nch + k)
        stage5(d)
      pn_ref[gx, 0:SUBLANES, xbc_l] = pc_ref[gx, last_rows, xbc_l]

    def out_pieces(p_ref, row0, ob):
        prow = slice(SUBLANES, SUBLANES + T)
        xrows = slice(row0, row0 + T)
        nq = D_MODEL // PW

        def gate_cols(name_lo, name_hi, split, j):
            c0 = j * PW
            name, off = (name_lo, c0) if c0 < split else (name_hi, c0 - split)
            gi, l0 = SEG[name]
            return p_ref[gi, prow, l0 + off:l0 + off + PW]

        def merge(j):
            def emit():
                cols = slice(j * PW, (j + 1) * PW)
                ga = gate_cols("ga0", "ga1", GA_SPLIT, j)
                gb = gate_cols("gb", "gb", D_MODEL, j)
                m = (_sigmoid(ga) * _dot(o_buf[ob], wbg_ref[:, cols])
                     + _sigmoid(gb) * _dot(y_buf[ob], wbs_ref[:, cols]))
                mg_buf[:, cols] = m.astype(BF16)
            return emit

        def project(j):
            def emit():
                cols = slice(j * PW, (j + 1) * PW)
                y_ref[xrows, cols] = x_ref[xrows, cols] + _dot(mg_buf[...], wout_ref[:, cols])
            return emit

        def final_norm():
            for r0 in range(row0, row0 + T, c):
                v = y_ref[r0:r0 + c, :]
                y_ref[r0:r0 + c, :] = v * _rms_scale(v) * fg_ref[...]

        return [merge(j) for j in range(nq)] + [project(j) for j in range(nq)] + [final_norm]

    bufs = (pa_ref, pb_ref)
    for blk in range(STEP_BLOCKS):
        if blk + 1 < STEP_BLOCKS:
            normalise(x_ref, (blk + 1) * T)
        else:
            normalise(xn_ref, 0)
        cur, nxt = bufs[blk % 2], bufs[(blk + 1) % 2]
        run_block(cur, nxt, blk % 2, [])
        for emit in out_pieces(cur, blk * T, blk % 2):
            emit()

    @pl.when(t == pl.num_programs(1) - 1)
    def _():
        for h in range(GLA_HEADS):
            blk = st_ref[h * GLA_DV:(h + 1) * GLA_DV, :].T
            st_out[h] = blk[h * GLA_DK:(h + 1) * GLA_DK, :]
        hct = hc_ref[...].T
        rep = SSD_HEADS // SSD_GROUPS
        for h in range(SSD_HEADS):
            blk = hct[h * SSD_HEADDIM:(h + 1) * SSD_HEADDIM, :]
            if h // rep:
                blk = pltpu.roll(blk, SSD_BC - (h // rep) * SSD_STATE, 1)
            hc_out[h] = blk[:, 0:SSD_STATE]
        conv_out[...] = pb_ref[gx, last_rows, xbc_l]


def _prompt_call(x, params, consts):
    B, L, _ = x.shape
    T = PROMPT_T
    sb = STEP_BLOCKS
    assert L % (sb * T) == 0 and sb % 2 == 0 and T == NG * CHUNK and CONV_W == 4 and SSD_GROUPS == 2
    nt = L // (sb * T)
    full = lambda a: pl.BlockSpec(a.shape, lambda b, t: (0,) * a.ndim, pipeline_mode=pl.Buffered(1))
    ins = [params[n] for n in _PARAM_ORDER] + [consts[n] for n in _CONST_ORDER]
    out_shape = (
        jax.ShapeDtypeStruct((B, L, D_MODEL), F32),
        jax.ShapeDtypeStruct((B, GLA_HEADS, GLA_DK, GLA_DV), F32),
        jax.ShapeDtypeStruct((B, SSD_HEADS, SSD_HEADDIM, SSD_STATE), F32),
        jax.ShapeDtypeStruct((B, SUBLANES, SSD_CONV_DIM), F32),
    )

    def next_block(b, t):
        lin = jnp.minimum(b * nt + t + 1, B * nt - 1)
        return (lin // nt, sb * (lin % nt), 0)

    tok = pl.BlockSpec((None, sb * T, D_MODEL), lambda b, t: (b, t, 0))
    tok_next = pl.BlockSpec((None, T, D_MODEL), next_block)
    per_b = lambda s: pl.BlockSpec((None,) + s, lambda b, t: (b,) + (0,) * len(s))
    return pl.pallas_call(
        functools.partial(_prompt_kernel, T),
        grid=(B, nt),
        in_specs=[tok, tok_next] + [full(a) for a in ins],
        out_specs=(tok, per_b((GLA_HEADS, GLA_DK, GLA_DV)), per_b((SSD_HEADS, SSD_HEADDIM, SSD_STATE)),
                   per_b((SUBLANES, SSD_CONV_DIM))),
        out_shape=out_shape,
        scratch_shapes=[
            pltpu.VMEM((NG, T + SUBLANES, GW), F32),
            pltpu.VMEM((NG, T + SUBLANES, GW), F32),
            pltpu.VMEM((T, D_MODEL), BF16),
            pltpu.VMEM((2, T, GLA_V), BF16),
            pltpu.VMEM((2, T, SSD_INNER), BF16),
            pltpu.VMEM((T, D_MODEL), BF16),
            pltpu.VMEM((GLA_V, GLA_QK), F32),
            pltpu.VMEM((SSD_BC, SSD_INNER), F32),
        ],
        compiler_params=pltpu.CompilerParams(dimension_semantics=("arbitrary", "arbitrary"),
                                             vmem_limit_bytes=VMEM_LIMIT),
        name="prompt_step",
    )(x, x, *ins)


def _sample_kernel(NB, x_ref, sg_ref, ss_ref, sc_ref, wm_ref, wg_ref, wx_ref, wa2_ref, wbg_ref, wbs_ref, wout_ref,
                   ng_ref, ba2_ref, glag_ref, cw_ref, cb_ref, dtb_ref, alog_ref, dsk_ref, ssdg_ref, fg_ref,
                   y_ref, sg_out, ss_out, sc_out,
                   p_ref, hn_ref, qt_ref, kt_ref, ebt_ref, xdtt_ref, elat_ref, v_ref, xs_ref, bt_ref, ct_ref,
                   o_ref, yt_ref):
    i = pl.program_id(0)
    sb = SAMPLE_SB
    cd = SSD_CONV_DIM
    rep = SSD_HEADS // SSD_GROUPS

    def seg(name, width):
        gi, l0 = SEG[name]
        return p_ref[gi, :, l0:l0 + width]

    @pl.when(i == 0)
    def _():
        xb = x_ref[:, 0, :]
        hn_ref[...] = (xb * _rms_scale(xb) * ng_ref[...]).astype(BF16)
        def store(gi, l0, width, value):
            p_ref[gi, :, l0:l0 + width] = value
        for emit in _proj_pieces(lambda: hn_ref[...], dict(main=wm_ref, gates=wg_ref, misc=wx_ref), store):
            emit()
        misc = seg("misc", LANES)
        pre = _dot(misc.astype(BF16), wa2_ref[...]) + ba2_ref[...]
        log_a = _log_sigmoid(pre) * (1.0 / GLA_GATE_TEMP)
        qt_ref[...] = (seg("q", GLA_QK) * (GLA_DK ** -0.5)).T.astype(BF16)
        kt_ref[...] = seg("k", GLA_QK).T.astype(BF16)
        ebt_ref[...] = jnp.concatenate(_split3(jnp.exp(log_a).T), axis=1)
        v_ref[...] = seg("v", GLA_V)

        xbc = seg("xbc", cd)
        xc = xbc * cw_ref[CONV_W - 1:CONV_W, :] + cb_ref[...]
        for s in range(CONV_W - 1):
            xc = xc + sc_ref[s] * cw_ref[s:s + 1, :]
        for s in range(CONV_W - 2):
            sc_out[s] = sc_ref[s + 1]
        sc_out[CONV_W - 2] = xbc
        xc = xc * _sigmoid(xc)
        xs = xc[:, 0:SSD_INNER]
        xs_ref[...] = xs
        bt_ref[...] = xc[:, SSD_INNER:SSD_INNER + SSD_BC].T
        ct_ref[...] = xc[:, SSD_INNER + SSD_BC:].T
        dtc = _softplus(misc + dtb_ref[...])
        dtt = dtc.T
        elat_ref[...] = jnp.exp(dtc * _a_pad(alog_ref)).T
        xst = xs.T
        for h in range(SSD_HEADS):
            hp = slice(h * SSD_HEADDIM, (h + 1) * SSD_HEADDIM)
            xdtt_ref[hp, :] = xst[hp, :] * dtt[MISC_DT + h:MISC_DT + h + 1, :]

    head = i // (SSD_HEADDIM // SSD_PB)
    grp = pl.multiple_of((head // rep) * SSD_STATE, SSD_STATE)
    ela = elat_ref[pl.ds(MISC_DT + head, 1), :]
    btg = bt_ref[pl.ds(grp, SSD_STATE), :]
    ctg = ct_ref[pl.ds(grp, SSD_STATE), :]
    prow = pl.ds(pl.multiple_of(i * SSD_PB, SSD_PB), SSD_PB)
    xrows = xdtt_ref[prow, :]
    p_iota = lax.broadcasted_iota(jnp.int32, (SSD_PB, 1), 0)
    y_rows = jnp.zeros((SSD_PB, NB), F32)
    for pp in range(SSD_PB):
        h_new = ela * ss_ref[0, pp] + xrows[pp:pp + 1, :] * btg
        ss_out[0, pp] = h_new
        y_rows = jnp.where(p_iota == pp, jnp.sum(ctg * h_new, axis=0, keepdims=True), y_rows)
    yt_ref[prow, :] = y_rows

    blk = pl.ds(pl.multiple_of(i * sb, sb), sb)
    vblk = v_ref[blk, :]
    row_iota = lax.broadcasted_iota(jnp.int32, (NB, LANES), 0)
    sub_iota = lax.broadcasted_iota(jnp.int32, (sb, 1), 0)
    o_blk = jnp.zeros((sb, GLA_V), F32)
    for s in range(sb):
        sel = (row_iota == i * sb + s).astype(BF16)
        sel3 = jnp.concatenate([sel] * 3, axis=0)
        qcol = _dot(qt_ref[...], sel)
        kcol = _dot(kt_ref[...], sel)
        ecol = _dot(ebt_ref[...], sel3)
        o_parts = []
        for h in range(GLA_HEADS):
            hd = slice(h * GLA_DK, (h + 1) * GLA_DK)
            s_new = ecol[hd] * sg_ref[s, h] + kcol[hd] * vblk[s:s + 1, h * GLA_DV:(h + 1) * GLA_DV]
            sg_out[s, h] = s_new
            o_parts.append(jnp.sum(qcol[hd] * s_new, axis=0, keepdims=True))
        o_blk = jnp.where(sub_iota == s, jnp.concatenate(o_parts, axis=-1), o_blk)
    o_ref[blk, :] = o_blk

    @pl.when(i == pl.num_programs(0) - 1)
    def _():
        g = seg("g", GLA_V)
        o = _head_rmsnorm(o_ref[...], GLA_DV) * glag_ref[...] * (g * _sigmoid(g))
        z = seg("z", SSD_INNER)
        y = yt_ref[...].T + dsk_ref[...] * xs_ref[...]
        y = _head_rmsnorm(y * (z * _sigmoid(z)), SSD_INNER // SSD_GROUPS) * ssdg_ref[...]
        gate_a = jnp.concatenate([seg("ga0", GA_SPLIT), seg("ga1", D_MODEL - GA_SPLIT)], axis=-1)
        y_ref[:, 0, :] = _merge_out(x_ref[:, 0, :], o.astype(BF16), y.astype(BF16), gate_a, seg("gb", D_MODEL),
                                wbg_ref, wbs_ref, wout_ref, fg_ref)


def _sample_call(x, state_gla, state_ssm, state_conv, params, consts):
    NB = x.shape[0]
    sb = SAMPLE_SB
    steps = NB // sb
    assert NB == LANES and NB % sb == 0 and steps * SSD_PB == SSD_INNER
    assert x.shape == (NB, 1, D_MODEL)
    ss_t = jnp.transpose(state_ssm, (1, 2, 3, 0))
    sc_t = jnp.transpose(state_conv, (1, 0, 2))
    ins = [params[n] for n in _PARAM_ORDER]
    full = lambda a: pl.BlockSpec(a.shape, lambda i: (0,) * a.ndim)
    sg_spec = pl.BlockSpec((sb, GLA_HEADS, GLA_DK, GLA_DV), lambda i: (i, 0, 0, 0))
    pblocks = SSD_HEADDIM // SSD_PB
    ss_spec = pl.BlockSpec((1, SSD_PB, SSD_STATE, NB), lambda i: (i // pblocks, i % pblocks, 0, 0))
    out_shape = (
        jax.ShapeDtypeStruct(x.shape, F32),
        jax.ShapeDtypeStruct(state_gla.shape, F32),
        jax.ShapeDtypeStruct(ss_t.shape, F32),
        jax.ShapeDtypeStruct(sc_t.shape, F32),
    )
    y, sg, ss, sc = pl.pallas_call(
        functools.partial(_sample_kernel, NB),
        grid=(steps,),
        in_specs=[full(x), sg_spec, ss_spec, full(sc_t)] + [full(a) for a in ins],
        out_specs=(full(x), sg_spec, ss_spec, full(sc_t)),
        out_shape=out_shape,
        scratch_shapes=[
            pltpu.VMEM((NG, NB, GW), F32),
            pltpu.VMEM((NB, D_MODEL), BF16),
            pltpu.VMEM((GLA_QK, NB), BF16),
            pltpu.VMEM((GLA_QK, NB), BF16),
            pltpu.VMEM((GLA_QK, 3 * NB), BF16),
            pltpu.VMEM((SSD_INNER, NB), F32),
            pltpu.VMEM((LANES, NB), F32),
            pltpu.VMEM((NB, GLA_V), F32),
            pltpu.VMEM((NB, SSD_INNER), F32),
            pltpu.VMEM((SSD_BC, NB), F32),
            pltpu.VMEM((SSD_BC, NB), F32),
            pltpu.VMEM((NB, GLA_V), F32),
            pltpu.VMEM((SSD_INNER, NB), F32),
        ],
        compiler_params=pltpu.CompilerParams(dimension_semantics=("arbitrary",), vmem_limit_bytes=VMEM_LIMIT),
        name="decode_step",
    )(x, state_gla, ss_t, sc_t, *ins)
    return y, sg, jnp.transpose(ss, (3, 0, 1, 2)), jnp.transpose(sc, (1, 0, 2))


def kernel(x_prompt, x_sample, state_gla, state_ssm, state_conv, norm_g, w_in, w_a2, b_a2, gla_norm_g, conv_w, conv_b,
           dt_bias, a_log, d_skip, ssd_norm_g, w_br_gla, w_br_ssd, w_out, final_norm_g):
    params = _pack_params(norm_g[0], w_in[0], w_a2[0], b_a2[0], gla_norm_g[0], conv_w[0], conv_b[0], dt_bias[0],
                          a_log[0], d_skip[0], ssd_norm_g[0], w_br_gla[0], w_br_ssd[0], w_out[0], final_norm_g)
    consts = _consts()
    y_p, gla_p, ssm_p, conv_p = _prompt_call(x_prompt, params, consts)
    conv_p = conv_p[:, SUBLANES - (CONV_W - 1):, :]
    y_s, gla_s, ssm_s, conv_s = _sample_call(x_sample, state_gla[0], state_ssm[0], state_conv[0], params, consts)
    return (y_p, y_s, gla_p[None], ssm_p[None], conv_p[None], gla_s[None], ssm_s[None], conv_s[None])
```

```python
import functools

import numpy as np
import jax
import jax.numpy as jnp
from jax import lax
from jax.experimental import pallas as pl
from jax.experimental.pallas import tpu as pltpu

F32 = jnp.float32
BF16 = jnp.bfloat16

D_MODEL = 1024
GLA_HEADS = 4
GLA_DK = 64
GLA_DV = 128
GLA_QK = GLA_HEADS * GLA_DK
GLA_V = GLA_HEADS * GLA_DV
GLA_RANK = 16
GLA_GATE_TEMP = 16.0
SSD_HEADS = 8
SSD_HEADDIM = 64
SSD_INNER = SSD_HEADS * SSD_HEADDIM
SSD_GROUPS = 2
SSD_STATE = 64
SSD_BC = SSD_GROUPS * SSD_STATE
CONV_W = 4
SSD_CONV_DIM = SSD_INNER + 2 * SSD_BC
CHUNK = 64
EPS = 1e-6
PROJ_SIZES = (GLA_QK, GLA_QK, GLA_V, GLA_V, GLA_RANK, SSD_INNER, SSD_CONV_DIM, SSD_HEADS, D_MODEL, D_MODEL)

LANES = 128
SUBLANES = 8

NG = 4
GW = 1280
PW = 256
GA_SPLIT = GW - SSD_INNER
SEG = dict(q=(0, 0), k=(0, GLA_QK), v=(0, 2 * GLA_QK), misc=(0, 2 * GLA_QK + GLA_V),
           g=(1, 0), xbc=(1, GLA_V),
           z=(2, 0), ga0=(2, SSD_INNER),
           ga1=(3, 0), gb=(3, D_MODEL - GA_SPLIT))
MISC_DT = GLA_RANK

_OFF = dict(zip(("q", "k", "v", "g", "a_lr", "z", "xbc", "dt", "ga", "gb"),
                (int(o) for o in np.cumsum((0,) + PROJ_SIZES[:-1]))))
_SRC = dict(q=("main", _OFF["q"], GLA_QK), k=("main", _OFF["k"], GLA_QK), v=("main", _OFF["v"], GLA_V),
            misc=("misc", 0, LANES), g=("main", _OFF["g"], GLA_V), xbc=("main", _OFF["xbc"], SSD_CONV_DIM),
            z=("main", _OFF["z"], SSD_INNER), ga0=("gates", 0, GA_SPLIT), ga1=("gates", GA_SPLIT, D_MODEL - GA_SPLIT),
            gb=("gates", D_MODEL, D_MODEL))
PROJ_PW = 4 * PW
_PIECES = [(SEG[n][0], SEG[n][1] + o, min(PROJ_PW, size - o), src, row + o)
           for n, (src, row, size) in _SRC.items() for o in range(0, size, PROJ_PW)]

XH = 2
PROMPT_T = 256
STEP_BLOCKS = 2
SAMPLE_SB = 16
SSD_PB = 64
VMEM_LIMIT = 56 * 1024 * 1024


def _split3(x):
    hi = x.astype(BF16)
    r1 = x - hi.astype(F32)
    mid = r1.astype(BF16)
    lo = (r1 - mid.astype(F32)).astype(BF16)
    return hi, mid, lo


def _sigmoid(x):
    return 1.0 / (1.0 + jnp.exp(-x))


def _softplus(x):
    return jnp.maximum(x, 0.0) + jnp.log1p(jnp.exp(-jnp.abs(x)))


def _rms_scale(x):
    return lax.rsqrt(jnp.mean(x * x, axis=-1, keepdims=True) + EPS)


def _dot(a, b):
    return jnp.dot(a, b, preferred_element_type=F32)


def _dot_nt(a, b):
    return lax.dot_general(a, b, (((1,), (1,)), ((), ())), preferred_element_type=F32)


def _dot_tn(a, b):
    return lax.dot_general(a, b, (((0,), (0,)), ((), ())), preferred_element_type=F32)


def _consts():
    r = np.arange

    def m(a, dt):
        return jnp.asarray(a.astype(np.float32), dtype=dt)

    c = CHUNK
    return dict(
        tril3=m(r(c)[:, None] >= (r(3 * c)[None, :] % c), BF16),
        km=m(r(GLA_HEADS * c)[:, None] // c == r(GLA_QK)[None, :] // GLA_DK, BF16),
        tril4=m(r(c)[:, None] >= r(GLA_HEADS * c)[None, :] % c, F32),
        vm=m(r(GLA_HEADS * c)[:, None] // c == r(GLA_V)[None, :] // GLA_DV, BF16),
        stm=m(r(GLA_V)[:, None] // GLA_DV == r(GLA_QK)[None, :] // GLA_DK, F32),
        bm=m(r(SSD_HEADS * c)[:, None] // (c * SSD_HEADS // SSD_GROUPS) == r(SSD_BC)[None, :] // SSD_STATE, BF16),
        tril8=m(r(c)[:, None] >= r(SSD_HEADS * c)[None, :] % c, F32),
        diag8=m(r(c)[:, None] == r(SSD_HEADS * c)[None, :] % c, F32),
        xm=m(r(XH * c)[:, None] // c == r(XH * SSD_HEADDIM)[None, :] // SSD_HEADDIM, BF16),
        hm=m(r(SSD_BC)[:, None] // SSD_STATE == r(SSD_INNER)[None, :] // (SSD_INNER // SSD_GROUPS), F32),
        e2=m((r(2 * LANES)[:, None] % LANES) == MISC_DT + r(SSD_INNER)[None, :] // SSD_HEADDIM, BF16),
    )


_CONST_ORDER = ("tril3", "km", "tril4", "vm", "stm", "bm", "tril8", "diag8", "xm", "hm", "e2")


def _pack_params(norm_g, w_in, w_a2, b_a2, gla_norm_g, conv_w, conv_b, dt_bias, a_log, d_skip, ssd_norm_g,
                 w_br_gla, w_br_ssd, w_out, final_norm_g):
    wt = w_in.T
    wm = wt[:_OFF["dt"]].astype(BF16)
    wg = wt[_OFF["ga"]:].astype(BF16)
    wx = jnp.concatenate([wt[_OFF["a_lr"]:_OFF["z"]], wt[_OFF["dt"]:_OFF["ga"]],
                          jnp.zeros((LANES - GLA_RANK - SSD_HEADS, D_MODEL), wt.dtype)], axis=0).astype(BF16)
    wa2 = jnp.concatenate([w_a2, jnp.zeros((LANES - GLA_RANK, GLA_QK), w_a2.dtype)], axis=0).astype(BF16)

    def misc_row(p):
        return jnp.zeros((1, LANES), F32).at[0, MISC_DT:MISC_DT + SSD_HEADS].set(p.astype(F32))

    return dict(
        wm=wm, wg=wg, wx=wx, wa2=wa2,
        wbg=w_br_gla.astype(BF16), wbs=w_br_ssd.astype(BF16), wout=w_out.astype(BF16),
        ng=norm_g.reshape(1, D_MODEL), ba2=b_a2.reshape(1, GLA_QK), glag=gla_norm_g.reshape(1, GLA_V),
        cw=conv_w, cb=conv_b.reshape(1, SSD_CONV_DIM), dtb=misc_row(dt_bias), alog=misc_row(a_log),
        dsk=jnp.repeat(d_skip.astype(F32), SSD_HEADDIM).reshape(1, SSD_INNER),
        ssdg=ssd_norm_g.reshape(1, SSD_INNER), fg=final_norm_g.reshape(1, D_MODEL),
    )


_PARAM_ORDER = ("wm", "wg", "wx", "wa2", "wbg", "wbs", "wout",
                "ng", "ba2", "glag", "cw", "cb", "dtb", "alog", "dsk", "ssdg", "fg")


def _proj_pieces(hn, w_refs, store):
    def piece(gi, l0, width, src, row):
        def emit():
            store(gi, l0, width, _dot_nt(hn(), w_refs[src][row:row + width, :]))
        return emit
    return [piece(*p) for p in _PIECES]


def _a_pad(alog_ref):
    lane = lax.broadcasted_iota(jnp.int32, (1, LANES), 1)
    is_dt = (lane >= MISC_DT) & (lane < MISC_DT + SSD_HEADS)
    return jnp.where(is_dt, -jnp.exp(alog_ref[...]), 0.0)


def _log_sigmoid(x):
    return jnp.minimum(x, 0.0) - jnp.log(1.0 + jnp.exp(-jnp.abs(x)))


def _merge_out(x, o_bf, y_bf, gate_a, gate_b, wbg_ref, wbs_ref, wout_ref, fg_ref):
    merged = _sigmoid(gate_a) * _dot(o_bf, wbg_ref[...]) + _sigmoid(gate_b) * _dot(y_bf, wbs_ref[...])
    out = x + _dot(merged.astype(BF16), wout_ref[...])
    return out * _rms_scale(out) * fg_ref[...]


def _head_rmsnorm(x, width):
    parts = []
    for c0 in range(0, x.shape[-1], width):
        blk = x[:, c0:c0 + width]
        parts.append(blk * _rms_scale(blk))
    return jnp.concatenate(parts, axis=-1)


def _prompt_kernel(T, x_ref, xn_ref, wm_ref, wg_ref, wx_ref, wa2_ref, wbg_ref, wbs_ref, wout_ref,
                   ng_ref, ba2_ref, glag_ref, cw_ref, cb_ref, dtb_ref, alog_ref, dsk_ref, ssdg_ref, fg_ref,
                   tril3_ref, km_ref, tril4_ref, vm_ref, stm_ref, bm_ref, tril8_ref, diag8_ref, xm_ref, hm_ref, e2_ref,
                   y_ref, st_out, hc_out, conv_out,
                   pa_ref, pb_ref, hn_ref, o_buf, y_buf, mg_buf, st_ref, hc_ref):
    t = pl.program_id(1)
    step = pl.program_id(0) * pl.num_programs(1) + t
    c = CHUNK
    tok = slice(SUBLANES, SUBLANES + T)
    last_rows = slice(T, T + SUBLANES)
    gx, lx = SEG["xbc"]
    xbc_l = slice(lx, lx + SSD_CONV_DIM)

    def normalise(src_ref, row0):
        for r0 in range(0, T, c):
            xb = src_ref[row0 + r0:row0 + r0 + c, :]
            hn_ref[r0:r0 + c, :] = (xb * _rms_scale(xb) * ng_ref[...]).astype(BF16)

    w_refs = dict(main=wm_ref, gates=wg_ref, misc=wx_ref)

    def projection(dst_ref):
        def store(gi, l0, width, value):
            dst_ref[gi, tok, l0:l0 + width] = value
        return _proj_pieces(lambda: hn_ref[...], w_refs, store)

    @pl.when(step == 0)
    def _():
        normalise(x_ref, 0)
        for emit in projection(pa_ref):
            emit()

    @pl.when(t == 0)
    def _():
        st_ref[...] = jnp.zeros_like(st_ref)
        hc_ref[...] = jnp.zeros_like(hc_ref)
        pa_ref[gx, 0:SUBLANES, xbc_l] = jnp.zeros((SUBLANES, SSD_CONV_DIM), F32)

    a_pad = _a_pad(alog_ref)

    def run_block(pc_ref, pn_ref, ob, deferred):
      def seg(name, rows, width):
        gi, l0 = SEG[name]
        return pc_ref[gi, rows, l0:l0 + width]


      def stage1(ci):
        r0 = ci * c
        rows = slice(r0 + SUBLANES, r0 + SUBLANES + c)
        misc = seg("misc", rows, LANES)
        pre = _dot(misc.astype(BF16), wa2_ref[...]) + ba2_ref[...]
        dtc = _softplus(misc + dtb_ref[...])
        cum_c = _dot(tril3_ref[:, 0:2 * c], jnp.concatenate(_split3(dtc * a_pad)[:2], axis=0))
        return dict(r0=r0, rows=rows, pre=pre, dtc=dtc, cum_c=cum_c)

      def stage2(d):
        log_a = _log_sigmoid(d["pre"]) * (1.0 / GLA_GATE_TEMP)
        d["b"] = _dot(tril3_ref[:, 0:2 * c], jnp.concatenate(_split3(log_a)[:2], axis=0))
        both = jnp.concatenate([jnp.concatenate(_split3(d["cum_c"])[:2], axis=1),
                                jnp.concatenate(_split3(d["dtc"])[:2], axis=1)], axis=0)
        d["ex"] = _dot(both, e2_ref[...])

      def stage3(d):
        rows, b = d["rows"], d["b"]
        b_last = b[c - 1:c, :]
        k = seg("k", rows, GLA_QK)
        qs = (seg("q", rows, GLA_QK) * (GLA_DK ** -0.5) * jnp.exp(b)).astype(BF16)
        ks = (k * jnp.exp(-b)).astype(BF16)
        kt = jnp.concatenate([ks] * GLA_HEADS, axis=0) * km_ref[...]
        d["a_cat"] = (_dot_nt(qs, kt) * tril4_ref[...]).astype(BF16)
        d["qs"] = qs
        d["kd"] = (k * jnp.exp(b_last - b)).astype(BF16)
        d["eb_last"] = jnp.exp(b_last)

        xe = pc_ref[gx, d["r0"]:d["r0"] + c + SUBLANES, xbc_l]
        x1 = pltpu.roll(xe, 1, 0)
        near = xe[SUBLANES:] * cw_ref[3:4, :] + x1[SUBLANES:] * cw_ref[2:3, :]
        far = pltpu.roll(xe * cw_ref[1:2, :] + x1 * cw_ref[0:1, :], 2, 0)[SUBLANES:]
        xc = near + far + cb_ref[...]
        xc = xc * _sigmoid(xc)
        xs = xc[:, 0:SSD_INNER]
        b_bf = xc[:, SSD_INNER:SSD_INNER + SSD_BC].astype(BF16)
        c_bf = xc[:, SSD_INNER + SSD_BC:].astype(BF16)
        bt = jnp.concatenate([b_bf] * SSD_HEADS, axis=0) * bm_ref[...]
        cum_e = d["ex"][0:c]
        cum_row = jnp.sum(cum_e * diag8_ref[...], axis=0, keepdims=True)
        decay = jnp.exp(jnp.minimum(cum_e - cum_row, 0.0)) * tril8_ref[...]
        d["m_bf"] = (_dot_nt(c_bf, bt) * decay).astype(BF16)
        xdt = xs * d["ex"][c:2 * c]
        c_last = cum_e[c - 1:c, :]
        d["xd"] = (xdt * jnp.exp(c_last - cum_e)).astype(BF16)
        d["xdt_bf"] = xdt.astype(BF16)
        d["ec_last"] = jnp.exp(c_last)
        d["ecum"] = jnp.exp(cum_e)
        d["xs"], d["b_bf"], d["c_bf"] = xs, b_bf, c_bf

      def stage4(d):
        v_bf = seg("v", d["rows"], GLA_V).astype(BF16)
        vbd = jnp.concatenate([v_bf] * GLA_HEADS, axis=0) * vm_ref[...]
        st = st_ref[...]
        d["o"] = _dot(d["a_cat"], vbd) + _dot_nt(d["qs"], st.astype(BF16))
        st_ref[...] = st * d["eb_last"] + _dot_tn(v_bf, d["kd"]) * stm_ref[...]
        ys = []
        for quad in range(SSD_HEADS // XH):
            cols = slice(quad * XH * SSD_HEADDIM, (quad + 1) * XH * SSD_HEADDIM)
            xbd = jnp.concatenate([d["xdt_bf"][:, cols]] * XH, axis=0) * xm_ref[...]
            ys.append(_dot(d["m_bf"][:, cols], xbd))
        hc = hc_ref[...]
        d["y"] = (jnp.concatenate(ys, axis=-1) + _dot(d["c_bf"], hc.astype(BF16)) * d["ecum"]
                  + dsk_ref[...] * d["xs"])
        hc_ref[...] = hc * d["ec_last"] + _dot_tn(d["b_bf"], d["xd"]) * hm_ref[...]

      def stage5(d):
        g = seg("g", d["rows"], GLA_V)
        o = _head_rmsnorm(d["o"], GLA_DV) * glag_ref[...] * (g * _sigmoid(g))
        o_buf[ob, pl.ds(d["r0"], c), :] = o.astype(BF16)
        z = seg("z", d["rows"], SSD_INNER)
        y = _head_rmsnorm(d["y"] * (z * _sigmoid(z)), SSD_INNER // SSD_GROUPS) * ssdg_ref[...]
        y_buf[ob, pl.ds(d["r0"], c), :] = y.astype(BF16)

      fill = projection(pn_ref) + deferred
      nch = T // c
      n_slots = 3 + 2 * nch
      it = iter(fill)

      counts = [0, 1, 1, 1, 1, 1, 1, 1, 1, 1, 1]
      assert sum(counts) == len(fill) and len(counts) == n_slots

      def gap(slot):
        for _ in range(counts[slot]):
            next(it)()

      gap(0)
      ds = [stage1(ci) for ci in range(nch)]
      gap(1)
      for d in ds:
        stage2(d)
      gap(2)
      for k, d in enumerate(ds):
        stage3(d)
        gap(3 + k)
      for k, d in enumerate(ds):
        stage4(d)
        gap(3 + nch + k)
        stage5(d)
      pn_ref[gx, 0:SUBLANES, xbc_l] = pc_ref[gx, last_rows, xbc_l]

    def out_pieces(p_ref, row0, ob):
        prow = slice(SUBLANES, SUBLANES + T)
        xrows = slice(row0, row0 + T)
        nq = D_MODEL // PW

        def gate_cols(name_lo, name_hi, split, j):
            c0 = j * PW
            name, off = (name_lo, c0) if c0 < split else (name_hi, c0 - split)
            gi, l0 = SEG[name]
            return p_ref[gi, prow, l0 + off:l0 + off + PW]

        def merge(j):
            def emit():
                cols = slice(j * PW, (j + 1) * PW)
                ga = gate_cols("ga0", "ga1", GA_SPLIT, j)
                gb = gate_cols("gb", "gb", D_MODEL, j)
                m = (_sigmoid(ga) * _dot(o_buf[ob], wbg_ref[:, cols])
                     + _sigmoid(gb) * _dot(y_buf[ob], wbs_ref[:, cols]))
                mg_buf[:, cols] = m.astype(BF16)
            return emit

        def project(j):
            def emit():
                cols = slice(j * PW, (j + 1) * PW)
                y_ref[xrows, cols] = x_ref[xrows, cols] + _dot(mg_buf[...], wout_ref[:, cols])
            return emit

        def final_norm():
            for r0 in range(row0, row0 + T, c):
                v = y_ref[r0:r0 + c, :]
                y_ref[r0:r0 + c, :] = v * _rms_scale(v) * fg_ref[...]

        return [merge(j) for j in range(nq)] + [project(j) for j in range(nq)] + [final_norm]

    bufs = (pa_ref, pb_ref)
    for blk in range(STEP_BLOCKS):
        if blk + 1 < STEP_BLOCKS:
            normalise(x_ref, (blk + 1) * T)
        else:
            normalise(xn_ref, 0)
        cur, nxt = bufs[blk % 2], bufs[(blk + 1) % 2]
        run_block(cur, nxt, blk % 2, [])
        for emit in out_pieces(cur, blk * T, blk % 2):
            emit()

    @pl.when(t == pl.num_programs(1) - 1)
    def _():
        for h in range(GLA_HEADS):
            blk = st_ref[h * GLA_DV:(h + 1) * GLA_DV, :].T
            st_out[h] = blk[h * GLA_DK:(h + 1) * GLA_DK, :]
        hct = hc_ref[...].T
        rep = SSD_HEADS // SSD_GROUPS
        for h in range(SSD_HEADS):
            blk = hct[h * SSD_HEADDIM:(h + 1) * SSD_HEADDIM, :]
            if h // rep:
                blk = pltpu.roll(blk, SSD_BC - (h // rep) * SSD_STATE, 1)
            hc_out[h] = blk[:, 0:SSD_STATE]
        conv_out[...] = pb_ref[gx, last_rows, xbc_l]


def _prompt_call(x, params, consts):
    B, L, _ = x.shape
    T = PROMPT_T
    sb = STEP_BLOCKS
    assert L % (sb * T) == 0 and sb % 2 == 0 and T == NG * CHUNK and CONV_W == 4 and SSD_GROUPS == 2
    nt = L // (sb * T)
    full = lambda a: pl.BlockSpec(a.shape, lambda b, t: (0,) * a.ndim, pipeline_mode=pl.Buffered(1))
    ins = [params[n] for n in _PARAM_ORDER] + [consts[n] for n in _CONST_ORDER]
    out_shape = (
        jax.ShapeDtypeStruct((B, L, D_MODEL), F32),
        jax.ShapeDtypeStruct((B, GLA_HEADS, GLA_DK, GLA_DV), F32),
        jax.ShapeDtypeStruct((B, SSD_HEADS, SSD_HEADDIM, SSD_STATE), F32),
        jax.ShapeDtypeStruct((B, SUBLANES, SSD_CONV_DIM), F32),
    )

    def next_block(b, t):
        lin = jnp.minimum(b * nt + t + 1, B * nt - 1)
        return (lin // nt, sb * (lin % nt), 0)

    tok = pl.BlockSpec((None, sb * T, D_MODEL), lambda b, t: (b, t, 0))
    tok_next = pl.BlockSpec((None, T, D_MODEL), next_block)
    per_b = lambda s: pl.BlockSpec((None,) + s, lambda b, t: (b,) + (0,) * len(s))
    return pl.pallas_call(
        functools.partial(_prompt_kernel, T),
        grid=(B, nt),
        in_specs=[tok, tok_next] + [full(a) for a in ins],
        out_specs=(tok, per_b((GLA_HEADS, GLA_DK, GLA_DV)), per_b((SSD_HEADS, SSD_HEADDIM, SSD_STATE)),
                   per_b((SUBLANES, SSD_CONV_DIM))),
        out_shape=out_shape,
        scratch_shapes=[
            pltpu.VMEM((NG, T + SUBLANES, GW), F32),
            pltpu.VMEM((NG, T + SUBLANES, GW), F32),
            pltpu.VMEM((T, D_MODEL), BF16),
            pltpu.VMEM((2, T, GLA_V), BF16),
            pltpu.VMEM((2, T, SSD_INNER), BF16),
            pltpu.VMEM((T, D_MODEL), BF16),
            pltpu.VMEM((GLA_V, GLA_QK), F32),
            pltpu.VMEM((SSD_BC, SSD_INNER), F32),
        ],
        compiler_params=pltpu.CompilerParams(dimension_semantics=("arbitrary", "arbitrary"),
                                             vmem_limit_bytes=VMEM_LIMIT),
        name="prompt_step",
    )(x, x, *ins)


def _sample_kernel(NB, x_ref, sg_ref, ss_ref, sc_ref, wm_ref, wg_ref, wx_ref, wa2_ref, wbg_ref, wbs_ref, wout_ref,
                   ng_ref, ba2_ref, glag_ref, cw_ref, cb_ref, dtb_ref, alog_ref, dsk_ref, ssdg_ref, fg_ref,
                   y_ref, sg_out, ss_out, sc_out,
                   p_ref, hn_ref, qt_ref, kt_ref, ebt_ref, xdtt_ref, elat_ref, v_ref, xs_ref, bt_ref, ct_ref,
                   o_ref, yt_ref):
    i = pl.program_id(0)
    sb = SAMPLE_SB
    cd = SSD_CONV_DIM
    rep = SSD_HEADS // SSD_GROUPS

    def seg(name, width):
        gi, l0 = SEG[name]
        return p_ref[gi, :, l0:l0 + width]

    @pl.when(i == 0)
    def _():
        xb = x_ref[:, 0, :]
        hn_ref[...] = (xb * _rms_scale(xb) * ng_ref[...]).astype(BF16)
        def store(gi, l0, width, value):
            p_ref[gi, :, l0:l0 + width] = value
        for emit in _proj_pieces(lambda: hn_ref[...], dict(main=wm_ref, gates=wg_ref, misc=wx_ref), store):
            emit()
        misc = seg("misc", LANES)
        pre = _dot(misc.astype(BF16), wa2_ref[...]) + ba2_ref[...]
        log_a = _log_sigmoid(pre) * (1.0 / GLA_GATE_TEMP)
        qt_ref[...] = (seg("q", GLA_QK) * (GLA_DK ** -0.5)).T.astype(BF16)
        kt_ref[...] = seg("k", GLA_QK).T.astype(BF16)
        ebt_ref[...] = jnp.concatenate(_split3(jnp.exp(log_a).T), axis=1)
        v_ref[...] = seg("v", GLA_V)

        xbc = seg("xbc", cd)
        xc = xbc * cw_ref[CONV_W - 1:CONV_W, :] + cb_ref[...]
        for s in range(CONV_W - 1):
            xc = xc + sc_ref[s] * cw_ref[s:s + 1, :]
        for s in range(CONV_W - 2):
            sc_out[s] = sc_ref[s + 1]
        sc_out[CONV_W - 2] = xbc
        xc = xc * _sigmoid(xc)
        xs = xc[:, 0:SSD_INNER]
        xs_ref[...] = xs
        bt_ref[...] = xc[:, SSD_INNER:SSD_INNER + SSD_BC].T
        ct_ref[...] = xc[:, SSD_INNER + SSD_BC:].T
        dtc = _softplus(misc + dtb_ref[...])
        dtt = dtc.T
        elat_ref[...] = jnp.exp(dtc * _a_pad(alog_ref)).T
        xst = xs.T
        for h in range(SSD_HEADS):
            hp = slice(h * SSD_HEADDIM, (h + 1) * SSD_HEADDIM)
            xdtt_ref[hp, :] = xst[hp, :] * dtt[MISC_DT + h:MISC_DT + h + 1, :]

    head = i // (SSD_HEADDIM // SSD_PB)
    grp = pl.multiple_of((head // rep) * SSD_STATE, SSD_STATE)
    ela = elat_ref[pl.ds(MISC_DT + head, 1), :]
    btg = bt_ref[pl.ds(grp, SSD_STATE), :]
    ctg = ct_ref[pl.ds(grp, SSD_STATE), :]
    prow = pl.ds(pl.multiple_of(i * SSD_PB, SSD_PB), SSD_PB)
    xrows = xdtt_ref[prow, :]
    p_iota = lax.broadcasted_iota(jnp.int32, (SSD_PB, 1), 0)
    y_rows = jnp.zeros((SSD_PB, NB), F32)
    for pp in range(SSD_PB):
        h_new = ela * ss_ref[0, pp] + xrows[pp:pp + 1, :] * btg
        ss_out[0, pp] = h_new
        y_rows = jnp.where(p_iota == pp, jnp.sum(ctg * h_new, axis=0, keepdims=True), y_rows)
    yt_ref[prow, :] = y_rows

    blk = pl.ds(pl.multiple_of(i * sb, sb), sb)
    vblk = v_ref[blk, :]
    row_iota = lax.broadcasted_iota(jnp.int32, (NB, LANES), 0)
    sub_iota = lax.broadcasted_iota(jnp.int32, (sb, 1), 0)
    o_blk = jnp.zeros((sb, GLA_V), F32)
    for s in range(sb):
        sel = (row_iota == i * sb + s).astype(BF16)
        sel3 = jnp.concatenate([sel] * 3, axis=0)
        qcol = _dot(qt_ref[...], sel)
        kcol = _dot(kt_ref[...], sel)
        ecol = _dot(ebt_ref[...], sel3)
        o_parts = []
        for h in range(GLA_HEADS):
            hd = slice(h * GLA_DK, (h + 1) * GLA_DK)
            s_new = ecol[hd] * sg_ref[s, h] + kcol[hd] * vblk[s:s + 1, h * GLA_DV:(h + 1) * GLA_DV]
            sg_out[s, h] = s_new
            o_parts.append(jnp.sum(qcol[hd] * s_new, axis=0, keepdims=True))
        o_blk = jnp.where(sub_iota == s, jnp.concatenate(o_parts, axis=-1), o_blk)
    o_ref[blk, :] = o_blk

    @pl.when(i == pl.num_programs(0) - 1)
    def _():
        g = seg("g", GLA_V)
        o = _head_rmsnorm(o_ref[...], GLA_DV) * glag_ref[...] * (g * _sigmoid(g))
        z = seg("z", SSD_INNER)
        y = yt_ref[...].T + dsk_ref[...] * xs_ref[...]
        y = _head_rmsnorm(y * (z * _sigmoid(z)), SSD_INNER // SSD_GROUPS) * ssdg_ref[...]
        gate_a = jnp.concatenate([seg("ga0", GA_SPLIT), seg("ga1", D_MODEL - GA_SPLIT)], axis=-1)
        y_ref[:, 0, :] = _merge_out(x_ref[:, 0, :], o.astype(BF16), y.astype(BF16), gate_a, seg("gb", D_MODEL),
                                wbg_ref, wbs_ref, wout_ref, fg_ref)


def _sample_call(x, state_gla, state_ssm, state_conv, params, consts):
    NB = x.shape[0]
    sb = SAMPLE_SB
    steps = NB // sb
    assert NB == LANES and NB % sb == 0 and steps * SSD_PB == SSD_INNER
    assert x.shape == (NB, 1, D_MODEL)
    ss_t = jnp.transpose(state_ssm, (1, 2, 3, 0))
    sc_t = jnp.transpose(state_conv, (1, 0, 2))
    ins = [params[n] for n in _PARAM_ORDER]
    full = lambda a: pl.BlockSpec(a.shape, lambda i: (0,) * a.ndim)
    sg_spec = pl.BlockSpec((sb, GLA_HEADS, GLA_DK, GLA_DV), lambda i: (i, 0, 0, 0))
    pblocks = SSD_HEADDIM // SSD_PB
    ss_spec = pl.BlockSpec((1, SSD_PB, SSD_STATE, NB), lambda i: (i // pblocks, i % pblocks, 0, 0))
    out_shape = (
        jax.ShapeDtypeStruct(x.shape, F32),
        jax.ShapeDtypeStruct(state_gla.shape, F32),
        jax.ShapeDtypeStruct(ss_t.shape, F32),
        jax.ShapeDtypeStruct(sc_t.shape, F32),
    )
    y, sg, ss, sc = pl.pallas_call(
        functools.partial(_sample_kernel, NB),
        grid=(steps,),
        in_specs=[full(x), sg_spec, ss_spec, full(sc_t)] + [full(a) for a in ins],
        out_specs=(full(x), sg_spec, ss_spec, full(sc_t)),
        out_shape=out_shape,
        scratch_shapes=[
            pltpu.VMEM((NG, NB, GW), F32),
            pltpu.VMEM((NB, D_MODEL), BF16),
            pltpu.VMEM((GLA_QK, NB), BF16),
            pltpu.VMEM((GLA_QK, NB), BF16),
            pltpu.VMEM((GLA_QK, 3 * NB), BF16),
            pltpu.VMEM((SSD_INNER, NB), F32),
            pltpu.VMEM((LANES, NB), F32),
            pltpu.VMEM((NB, GLA_V), F32),
            pltpu.VMEM((NB, SSD_INNER), F32),
            pltpu.VMEM((SSD_BC, NB), F32),
            pltpu.VMEM((SSD_BC, NB), F32),
            pltpu.VMEM((NB, GLA_V), F32),
            pltpu.VMEM((SSD_INNER, NB), F32),
        ],
        compiler_params=pltpu.CompilerParams(dimension_semantics=("arbitrary",), vmem_limit_bytes=VMEM_LIMIT),
        name="decode_step",
    )(x, state_gla, ss_t, sc_t, *ins)
    return y, sg, jnp.transpose(ss, (3, 0, 1, 2)), jnp.transpose(sc, (1, 0, 2))


def kernel(x_prompt, x_sample, state_gla, state_ssm, state_conv, norm_g, w_in, w_a2, b_a2, gla_norm_g, conv_w, conv_b,
           dt_bias, a_log, d_skip, ssd_norm_g, w_br_gla, w_br_ssd, w_out, final_norm_g):
    params = _pack_params(norm_g[0], w_in[0], w_a2[0], b_a2[0], gla_norm_g[0], conv_w[0], conv_b[0], dt_bias[0],
                          a_log[0], d_skip[0], ssd_norm_g[0], w_br_gla[0], w_br_ssd[0], w_out[0], final_norm_g)
    consts = _consts()
    y_p, gla_p, ssm_p, conv_p = _prompt_call(x_prompt, params, consts)
    conv_p = conv_p[:, SUBLANES - (CONV_W - 1):, :]
    y_s, gla_s, ssm_s, conv_s = _sample_call(x_sample, state_gla[0], state_ssm[0], state_conv[0], params, consts)
    return (y_p, y_s, gla_p[None], ssm_p[None], conv_p[None], gla_s[None], ssm_s[None], conv_s[None])
```

```python
import functools

import numpy as np
import jax
import jax.numpy as jnp
from jax import lax
from jax.experimental import pallas as pl
from jax.experimental.pallas import tpu as pltpu

F32 = jnp.float32
BF16 = jnp.bfloat16

D_MODEL = 1024
GLA_HEADS = 4
GLA_DK = 64
GLA_DV = 128
GLA_QK = GLA_HEADS * GLA_DK
GLA_V = GLA_HEADS * GLA_DV
GLA_RANK = 16
GLA_GATE_TEMP = 16.0
SSD_HEADS = 8
SSD_HEADDIM = 64
SSD_INNER = SSD_HEADS * SSD_HEADDIM
SSD_GROUPS = 2
SSD_STATE = 64
SSD_BC = SSD_GROUPS * SSD_STATE
CONV_W = 4
SSD_CONV_DIM = SSD_INNER + 2 * SSD_BC
CHUNK = 64
EPS = 1e-6
PROJ_SIZES = (GLA_QK, GLA_QK, GLA_V, GLA_V, GLA_RANK, SSD_INNER, SSD_CONV_DIM, SSD_HEADS, D_MODEL, D_MODEL)

LANES = 128
SUBLANES = 8

NG = 4
GW = 1280
PW = 256
GA_SPLIT = GW - SSD_INNER
SEG = dict(q=(0, 0), k=(0, GLA_QK), v=(0, 2 * GLA_QK), misc=(0, 2 * GLA_QK + GLA_V),
           g=(1, 0), xbc=(1, GLA_V),
           z=(2, 0), ga0=(2, SSD_INNER),
           ga1=(3, 0), gb=(3, D_MODEL - GA_SPLIT))
MISC_DT = GLA_RANK

_OFF = dict(zip(("q", "k", "v", "g", "a_lr", "z", "xbc", "dt", "ga", "gb"),
                (int(o) for o in np.cumsum((0,) + PROJ_SIZES[:-1]))))
_SRC = dict(q=("main", _OFF["q"], GLA_QK), k=("main", _OFF["k"], GLA_QK), v=("main", _OFF["v"], GLA_V),
            misc=("misc", 0, LANES), g=("main", _OFF["g"], GLA_V), xbc=("main", _OFF["xbc"], SSD_CONV_DIM),
            z=("main", _OFF["z"], SSD_INNER), ga0=("gates", 0, GA_SPLIT), ga1=("gates", GA_SPLIT, D_MODEL - GA_SPLIT),
            gb=("gates", D_MODEL, D_MODEL))
PROJ_PW = 4 * PW
_PIECES = [(SEG[n][0], SEG[n][1] + o, min(PROJ_PW, size - o), src, row + o)
           for n, (src, row, size) in _SRC.items() for o in range(0, size, PROJ_PW)]

XH = 2
PROMPT_T = 256
STEP_BLOCKS = 2
SAMPLE_SB = 16
SSD_PB = 64
VMEM_LIMIT = 56 * 1024 * 1024


def _split3(x):
    hi = x.astype(BF16)
    r1 = x - hi.astype(F32)
    mid = r1.astype(BF16)
    lo = (r1 - mid.astype(F32)).astype(BF16)
    return hi, mid, lo


def _sigmoid(x):
    return 1.0 / (1.0 + jnp.exp(-x))


def _softplus(x):
    return jnp.maximum(x, 0.0) + jnp.log1p(jnp.exp(-jnp.abs(x)))


def _rms_scale(x):
    return lax.rsqrt(jnp.mean(x * x, axis=-1, keepdims=True) + EPS)


def _dot(a, b):
    return jnp.dot(a, b, preferred_element_type=F32)


def _dot_nt(a, b):
    return lax.dot_general(a, b, (((1,), (1,)), ((), ())), preferred_element_type=F32)


def _dot_tn(a, b):
    return lax.dot_general(a, b, (((0,), (0,)), ((), ())), preferred_element_type=F32)


def _consts():
    r = np.arange

    def m(a, dt):
        return jnp.asarray(a.astype(np.float32), dtype=dt)

    c = CHUNK
    return dict(
        tril3=m(r(c)[:, None] >= (r(3 * c)[None, :] % c), BF16),
        km=m(r(GLA_HEADS * c)[:, None] // c == r(GLA_QK)[None, :] // GLA_DK, BF16),
        tril4=m(r(c)[:, None] >= r(GLA_HEADS * c)[None, :] % c, F32),
        vm=m(r(GLA_HEADS * c)[:, None] // c == r(GLA_V)[None, :] // GLA_DV, BF16),
        stm=m(r(GLA_V)[:, None] // GLA_DV == r(GLA_QK)[None, :] // GLA_DK, F32),
        bm=m(r(SSD_HEADS * c)[:, None] // (c * SSD_HEADS // SSD_GROUPS) == r(SSD_BC)[None, :] // SSD_STATE, BF16),
        tril8=m(r(c)[:, None] >= r(SSD_HEADS * c)[None, :] % c, F32),
        diag8=m(r(c)[:, None] == r(SSD_HEADS * c)[None, :] % c, F32),
        xm=m(r(XH * c)[:, None] // c == r(XH * SSD_HEADDIM)[None, :] // SSD_HEADDIM, BF16),
        hm=m(r(SSD_BC)[:, None] // SSD_STATE == r(SSD_INNER)[None, :] // (SSD_INNER // SSD_GROUPS), F32),
        e2=m((r(2 * LANES)[:, None] % LANES) == MISC_DT + r(SSD_INNER)[None, :] // SSD_HEADDIM, BF16),
    )


_CONST_ORDER = ("tril3", "km", "tril4", "vm", "stm", "bm", "tril8", "diag8", "xm", "hm", "e2")


def _pack_params(norm_g, w_in, w_a2, b_a2, gla_norm_g, conv_w, conv_b, dt_bias, a_log, d_skip, ssd_norm_g,
                 w_br_gla, w_br_ssd, w_out, final_norm_g):
    wt = w_in.T
    wm = wt[:_OFF["dt"]].astype(BF16)
    wg = wt[_OFF["ga"]:].astype(BF16)
    wx = jnp.concatenate([wt[_OFF["a_lr"]:_OFF["z"]], wt[_OFF["dt"]:_OFF["ga"]],
                          jnp.zeros((LANES - GLA_RANK - SSD_HEADS, D_MODEL), wt.dtype)], axis=0).astype(BF16)
    wa2 = jnp.concatenate([w_a2, jnp.zeros((LANES - GLA_RANK, GLA_QK), w_a2.dtype)], axis=0).astype(BF16)

    def misc_row(p):
        return jnp.zeros((1, LANES), F32).at[0, MISC_DT:MISC_DT + SSD_HEADS].set(p.astype(F32))

    return dict(
        wm=wm, wg=wg, wx=wx, wa2=wa2,
        wbg=w_br_gla.astype(BF16), wbs=w_br_ssd.astype(BF16), wout=w_out.astype(BF16),
        ng=norm_g.reshape(1, D_MODEL), ba2=b_a2.reshape(1, GLA_QK), glag=gla_norm_g.reshape(1, GLA_V),
        cw=conv_w, cb=conv_b.reshape(1, SSD_CONV_DIM), dtb=misc_row(dt_bias), alog=misc_row(a_log),
        dsk=jnp.repeat(d_skip.astype(F32), SSD_HEADDIM).reshape(1, SSD_INNER),
        ssdg=ssd_norm_g.reshape(1, SSD_INNER), fg=final_norm_g.reshape(1, D_MODEL),
    )


_PARAM_ORDER = ("wm", "wg", "wx", "wa2", "wbg", "wbs", "wout",
                "ng", "ba2", "glag", "cw", "cb", "dtb", "alog", "dsk", "ssdg", "fg")


def _proj_pieces(hn, w_refs, store):
    def piece(gi, l0, width, src, row):
        def emit():
            store(gi, l0, width, _dot_nt(hn(), w_refs[src][row:row + width, :]))
        return emit
    return [piece(*p) for p in _PIECES]


def _a_pad(alog_ref):
    lane = lax.broadcasted_iota(jnp.int32, (1, LANES), 1)
    is_dt = (lane >= MISC_DT) & (lane < MISC_DT + SSD_HEADS)
    return jnp.where(is_dt, -jnp.exp(alog_ref[...]), 0.0)


def _log_sigmoid(x):
    return jnp.minimum(x, 0.0) - jnp.log(1.0 + jnp.exp(-jnp.abs(x)))


def _merge_out(x, o_bf, y_bf, gate_a, gate_b, wbg_ref, wbs_ref, wout_ref, fg_ref):
    merged = _sigmoid(gate_a) * _dot(o_bf, wbg_ref[...]) + _sigmoid(gate_b) * _dot(y_bf, wbs_ref[...])
    out = x + _dot(merged.astype(BF16), wout_ref[...])
    return out * _rms_scale(out) * fg_ref[...]


def _head_rmsnorm(x, width):
    parts = []
    for c0 in range(0, x.shape[-1], width):
        blk = x[:, c0:c0 + width]
        parts.append(blk * _rms_scale(blk))
    return jnp.concatenate(parts, axis=-1)


def _prompt_kernel(T, x_ref, xn_ref, wm_ref, wg_ref, wx_ref, wa2_ref, wbg_ref, wbs_ref, wout_ref,
                   ng_ref, ba2_ref, glag_ref, cw_ref, cb_ref, dtb_ref, alog_ref, dsk_ref, ssdg_ref, fg_ref,
                   tril3_ref, km_ref, tril4_ref, vm_ref, stm_ref, bm_ref, tril8_ref, diag8_ref, xm_ref, hm_ref, e2_ref,
                   y_ref, st_out, hc_out, conv_out,
                   pa_ref, pb_ref, hn_ref, o_buf, y_buf, mg_buf, st_ref, hc_ref):
    t = pl.program_id(1)
    step = pl.program_id(0) * pl.num_programs(1) + t
    c = CHUNK
    tok = slice(SUBLANES, SUBLANES + T)
    last_rows = slice(T, T + SUBLANES)
    gx, lx = SEG["xbc"]
    xbc_l = slice(lx, lx + SSD_CONV_DIM)

    def normalise(src_ref, row0):
        for r0 in range(0, T, c):
            xb = src_ref[row0 + r0:row0 + r0 + c, :]
            hn_ref[r0:r0 + c, :] = (xb * _rms_scale(xb) * ng_ref[...]).astype(BF16)

    w_refs = dict(main=wm_ref, gates=wg_ref, misc=wx_ref)

    def projection(dst_ref):
        def store(gi, l0, width, value):
            dst_ref[gi, tok, l0:l0 + width] = value
        return _proj_pieces(lambda: hn_ref[...], w_refs, store)

    @pl.when(step == 0)
    def _():
        normalise(x_ref, 0)
        for emit in projection(pa_ref):
            emit()

    @pl.when(t == 0)
    def _():
        st_ref[...] = jnp.zeros_like(st_ref)
        hc_ref[...] = jnp.zeros_like(hc_ref)
        pa_ref[gx, 0:SUBLANES, xbc_l] = jnp.zeros((SUBLANES, SSD_CONV_DIM), F32)

    a_pad = _a_pad(alog_ref)

    def run_block(pc_ref, pn_ref, ob, deferred):
      def seg(name, rows, width):
        gi, l0 = SEG[name]
        return pc_ref[gi, rows, l0:l0 + width]


      def stage1(ci):
        r0 = ci * c
        rows = slice(r0 + SUBLANES, r0 + SUBLANES + c)
        misc = seg("misc", rows, LANES)
        pre = _dot(misc.astype(BF16), wa2_ref[...]) + ba2_ref[...]
        dtc = _softplus(misc + dtb_ref[...])
        cum_c = _dot(tril3_ref[:, 0:2 * c], jnp.concatenate(_split3(dtc * a_pad)[:2], axis=0))
        return dict(r0=r0, rows=rows, pre=pre, dtc=dtc, cum_c=cum_c)

      def stage2(d):
        log_a = _log_sigmoid(d["pre"]) * (1.0 / GLA_GATE_TEMP)
        d["b"] = _dot(tril3_ref[:, 0:2 * c], jnp.concatenate(_split3(log_a)[:2], axis=0))
        both = jnp.concatenate([jnp.concatenate(_split3(d["cum_c"])[:2], axis=1),
                                jnp.concatenate(_split3(d["dtc"])[:2], axis=1)], axis=0)
        d["ex"] = _dot(both, e2_ref[...])

      def stage3(d):
        rows, b = d["rows"], d["b"]
        b_last = b[c - 1:c, :]
        k = seg("k", rows, GLA_QK)
        q = seg("q", rows, GLA_QK) * (GLA_DK ** -0.5)
        b_mid = b[c // 2 - 1:c // 2, :]
        qm = (q * jnp.exp(b - b_mid)).astype(BF16)
        ks = (k * jnp.exp(b_mid - b)).astype(BF16)
        kt = jnp.concatenate([ks] * GLA_HEADS, axis=0) * km_ref[...]
        d["a_cat"] = (_dot_nt(qm, kt) * tril4_ref[...]).astype(BF16)
        d["qs"] = (q * jnp.exp(b)).astype(BF16)
        d["kd"] = (k * jnp.exp(b_last - b)).astype(BF16)
        d["eb_last"] = jnp.exp(b_last)

        xe = pc_ref[gx, d["r0"]:d["r0"] + c + SUBLANES, xbc_l]
        x1 = pltpu.roll(xe, 1, 0)
        near = xe[SUBLANES:] * cw_ref[3:4, :] + x1[SUBLANES:] * cw_ref[2:3, :]
        far = pltpu.roll(xe * cw_ref[1:2, :] + x1 * cw_ref[0:1, :], 2, 0)[SUBLANES:]
        xc = near + far + cb_ref[...]
        xc = xc * _sigmoid(xc)
        xs = xc[:, 0:SSD_INNER]
        b_bf = xc[:, SSD_INNER:SSD_INNER + SSD_BC].astype(BF16)
        c_bf = xc[:, SSD_INNER + SSD_BC:].astype(BF16)
        bt = jnp.concatenate([b_bf] * SSD_HEADS, axis=0) * bm_ref[...]
        cum_e = d["ex"][0:c]
        cum_row = jnp.sum(cum_e * diag8_ref[...], axis=0, keepdims=True)
        decay = jnp.exp(jnp.minimum(cum_e - cum_row, 0.0)) * tril8_ref[...]
        d["m_bf"] = (_dot_nt(c_bf, bt) * decay).astype(BF16)
        xdt = xs * d["ex"][c:2 * c]
        c_last = cum_e[c - 1:c, :]
        d["xd"] = (xdt * jnp.exp(c_last - cum_e)).astype(BF16)
        d["xdt_bf"] = xdt.astype(BF16)
        d["ec_last"] = jnp.exp(c_last)
        d["ecum"] = jnp.exp(cum_e)
        d["xs"], d["b_bf"], d["c_bf"] = xs, b_bf, c_bf

      def stage4(d):
        v_bf = seg("v", d["rows"], GLA_V).astype(BF16)
        vbd = jnp.concatenate([v_bf] * GLA_HEADS, axis=0) * vm_ref[...]
        st = st_ref[...]
        d["o"] = _dot(d["a_cat"], vbd) + _dot_nt(d["qs"], st.astype(BF16))
        st_ref[...] = st * d["eb_last"] + _dot_tn(v_bf, d["kd"]) * stm_ref[...]
        ys = []
        for quad in range(SSD_HEADS // XH):
            cols = slice(quad * XH * SSD_HEADDIM, (quad + 1) * XH * SSD_HEADDIM)
            xbd = jnp.concatenate([d["xdt_bf"][:, cols]] * XH, axis=0) * xm_ref[...]
            ys.append(_dot(d["m_bf"][:, cols], xbd))
        hc = hc_ref[...]
        d["y"] = (jnp.concatenate(ys, axis=-1) + _dot(d["c_bf"], hc.astype(BF16)) * d["ecum"]
                  + dsk_ref[...] * d["xs"])
        hc_ref[...] = hc * d["ec_last"] + _dot_tn(d["b_bf"], d["xd"]) * hm_ref[...]

      def stage5(d):
        g = seg("g", d["rows"], GLA_V)
        o = _head_rmsnorm(d["o"], GLA_DV) * glag_ref[...] * (g * _sigmoid(g))
        o_buf[ob, pl.ds(d["r0"], c), :] = o.astype(BF16)
        z = seg("z", d["rows"], SSD_INNER)
        y = _head_rmsnorm(d["y"] * (z * _sigmoid(z)), SSD_INNER // SSD_GROUPS) * ssdg_ref[...]
        y_buf[ob, pl.ds(d["r0"], c), :] = y.astype(BF16)

      fill = projection(pn_ref) + deferred
      nch = T // c
      n_slots = 3 + 2 * nch
      it = iter(fill)

      counts = [0, 1, 1, 1, 1, 1, 1, 1, 1, 1, 1]
      assert sum(counts) == len(fill) and len(counts) == n_slots

      def gap(slot):
        for _ in range(counts[slot]):
            next(it)()

      gap(0)
      ds = [stage1(ci) for ci in range(nch)]
      gap(1)
      for d in ds:
        stage2(d)
      gap(2)
      for k, d in enumerate(ds):
        stage3(d)
        gap(3 + k)
      for k, d in enumerate(ds):
        stage4(d)
        gap(3 + nch + k)
        stage5(d)
      pn_ref[gx, 0:SUBLANES, xbc_l] = pc_ref[gx, last_rows, xbc_l]

    def out_pieces(p_ref, row0, ob):
        prow = slice(SUBLANES, SUBLANES + T)
        xrows = slice(row0, row0 + T)
        nq = D_MODEL // PW

        def gate_cols(name_lo, name_hi, split, j):
            c0 = j * PW
            name, off = (name_lo, c0) if c0 < split else (name_hi, c0 - split)
            gi, l0 = SEG[name]
            return p_ref[gi, prow, l0 + off:l0 + off + PW]

        def merge(j):
            def emit():
                cols = slice(j * PW, (j + 1) * PW)
                ga = gate_cols("ga0", "ga1", GA_SPLIT, j)
                gb = gate_cols("gb", "gb", D_MODEL, j)
                m = (_sigmoid(ga) * _dot(o_buf[ob], wbg_ref[:, cols])
                     + _sigmoid(gb) * _dot(y_buf[ob], wbs_ref[:, cols]))
                mg_buf[:, cols] = m.astype(BF16)
            return emit

        def project(j):
            def emit():
                cols = slice(j * PW, (j + 1) * PW)
                y_ref[xrows, cols] = x_ref[xrows, cols] + _dot(mg_buf[...], wout_ref[:, cols])
            return emit

        def final_norm():
            for r0 in range(row0, row0 + T, c):
                v = y_ref[r0:r0 + c, :]
                y_ref[r0:r0 + c, :] = v * _rms_scale(v) * fg_ref[...]

        return [merge(j) for j in range(nq)] + [project(j) for j in range(nq)] + [final_norm]

    bufs = (pa_ref, pb_ref)
    for blk in range(STEP_BLOCKS):
        if blk + 1 < STEP_BLOCKS:
            normalise(x_ref, (blk + 1) * T)
        else:
            normalise(xn_ref, 0)
        cur, nxt = bufs[blk % 2], bufs[(blk + 1) % 2]
        run_block(cur, nxt, blk % 2, [])
        for emit in out_pieces(cur, blk * T, blk % 2):
            emit()

    @pl.when(t == pl.num_programs(1) - 1)
    def _():
        for h in range(GLA_HEADS):
            blk = st_ref[h * GLA_DV:(h + 1) * GLA_DV, :].T
            st_out[h] = blk[h * GLA_DK:(h + 1) * GLA_DK, :]
        hct = hc_ref[...].T
        rep = SSD_HEADS // SSD_GROUPS
        for h in range(SSD_HEADS):
            blk = hct[h * SSD_HEADDIM:(h + 1) * SSD_HEADDIM, :]
            if h // rep:
                blk = pltpu.roll(blk, SSD_BC - (h // rep) * SSD_STATE, 1)
            hc_out[h] = blk[:, 0:SSD_STATE]
        conv_out[...] = pb_ref[gx, last_rows, xbc_l]


def _prompt_call(x, params, consts):
    B, L, _ = x.shape
    T = PROMPT_T
    sb = STEP_BLOCKS
    assert L % (sb * T) == 0 and sb % 2 == 0 and T == NG * CHUNK and CONV_W == 4 and SSD_GROUPS == 2
    nt = L // (sb * T)
    full = lambda a: pl.BlockSpec(a.shape, lambda b, t: (0,) * a.ndim, pipeline_mode=pl.Buffered(1))
    ins = [params[n] for n in _PARAM_ORDER] + [consts[n] for n in _CONST_ORDER]
    out_shape = (
        jax.ShapeDtypeStruct((B, L, D_MODEL), F32),
        jax.ShapeDtypeStruct((B, GLA_HEADS, GLA_DK, GLA_DV), F32),
        jax.ShapeDtypeStruct((B, SSD_HEADS, SSD_HEADDIM, SSD_STATE), F32),
        jax.ShapeDtypeStruct((B, SUBLANES, SSD_CONV_DIM), F32),
    )

    def next_block(b, t):
        lin = jnp.minimum(b * nt + t + 1, B * nt - 1)
        return (lin // nt, sb * (lin % nt), 0)

    tok = pl.BlockSpec((None, sb * T, D_MODEL), lambda b, t: (b, t, 0))
    tok_next = pl.BlockSpec((None, T, D_MODEL), next_block)
    per_b = lambda s: pl.BlockSpec((None,) + s, lambda b, t: (b,) + (0,) * len(s))
    return pl.pallas_call(
        functools.partial(_prompt_kernel, T),
        grid=(B, nt),
        in_specs=[tok, tok_next] + [full(a) for a in ins],
        out_specs=(tok, per_b((GLA_HEADS, GLA_DK, GLA_DV)), per_b((SSD_HEADS, SSD_HEADDIM, SSD_STATE)),
                   per_b((SUBLANES, SSD_CONV_DIM))),
        out_shape=out_shape,
        scratch_shapes=[
            pltpu.VMEM((NG, T + SUBLANES, GW), F32),
            pltpu.VMEM((NG, T + SUBLANES, GW), F32),
            pltpu.VMEM((T, D_MODEL), BF16),
            pltpu.VMEM((2, T, GLA_V), BF16),
            pltpu.VMEM((2, T, SSD_INNER), BF16),
            pltpu.VMEM((T, D_MODEL), BF16),
            pltpu.VMEM((GLA_V, GLA_QK), F32),
            pltpu.VMEM((SSD_BC, SSD_INNER), F32),
        ],
        compiler_params=pltpu.CompilerParams(dimension_semantics=("arbitrary", "arbitrary"),
                                             vmem_limit_bytes=VMEM_LIMIT),
        name="prompt_step",
    )(x, x, *ins)


def _sample_kernel(NB, x_ref, sg_ref, ss_ref, sc_ref, wm_ref, wg_ref, wx_ref, wa2_ref, wbg_ref, wbs_ref, wout_ref,
                   ng_ref, ba2_ref, glag_ref, cw_ref, cb_ref, dtb_ref, alog_ref, dsk_ref, ssdg_ref, fg_ref,
                   y_ref, sg_out, ss_out, sc_out,
                   p_ref, hn_ref, qt_ref, kt_ref, ebt_ref, xdtt_ref, elat_ref, v_ref, xs_ref, bt_ref, ct_ref,
                   o_ref, yt_ref):
    i = pl.program_id(0)
    sb = SAMPLE_SB
    cd = SSD_CONV_DIM
    rep = SSD_HEADS // SSD_GROUPS

    def seg(name, width):
        gi, l0 = SEG[name]
        return p_ref[gi, :, l0:l0 + width]

    @pl.when(i == 0)
    def _():
        xb = x_ref[:, 0, :]
        hn_ref[...] = (xb * _rms_scale(xb) * ng_ref[...]).astype(BF16)
        def store(gi, l0, width, value):
            p_ref[gi, :, l0:l0 + width] = value
        for emit in _proj_pieces(lambda: hn_ref[...], dict(main=wm_ref, gates=wg_ref, misc=wx_ref), store):
            emit()
        misc = seg("misc", LANES)
        pre = _dot(misc.astype(BF16), wa2_ref[...]) + ba2_ref[...]
        log_a = _log_sigmoid(pre) * (1.0 / GLA_GATE_TEMP)
        qt_ref[...] = (seg("q", GLA_QK) * (GLA_DK ** -0.5)).T.astype(BF16)
        kt_ref[...] = seg("k", GLA_QK).T.astype(BF16)
        ebt_ref[...] = jnp.concatenate(_split3(jnp.exp(log_a).T), axis=1)
        v_ref[...] = seg("v", GLA_V)

        xbc = seg("xbc", cd)
        xc = xbc * cw_ref[CONV_W - 1:CONV_W, :] + cb_ref[...]
        for s in range(CONV_W - 1):
            xc = xc + sc_ref[s] * cw_ref[s:s + 1, :]
        for s in range(CONV_W - 2):
            sc_out[s] = sc_ref[s + 1]
        sc_out[CONV_W - 2] = xbc
        xc = xc * _sigmoid(xc)
        xs = xc[:, 0:SSD_INNER]
        xs_ref[...] = xs
        bt_ref[...] = xc[:, SSD_INNER:SSD_INNER + SSD_BC].T
        ct_ref[...] = xc[:, SSD_INNER + SSD_BC:].T
        dtc = _softplus(misc + dtb_ref[...])
        dtt = dtc.T
        elat_ref[...] = jnp.exp(dtc * _a_pad(alog_ref)).T
        xst = xs.T
        for h in range(SSD_HEADS):
            hp = slice(h * SSD_HEADDIM, (h + 1) * SSD_HEADDIM)
            xdtt_ref[hp, :] = xst[hp, :] * dtt[MISC_DT + h:MISC_DT + h + 1, :]

    head = i // (SSD_HEADDIM // SSD_PB)
    grp = pl.multiple_of((head // rep) * SSD_STATE, SSD_STATE)
    ela = elat_ref[pl.ds(MISC_DT + head, 1), :]
    btg = bt_ref[pl.ds(grp, SSD_STATE), :]
    ctg = ct_ref[pl.ds(grp, SSD_STATE), :]
    prow = pl.ds(pl.multiple_of(i * SSD_PB, SSD_PB), SSD_PB)
    xrows = xdtt_ref[prow, :]
    p_iota = lax.broadcasted_iota(jnp.int32, (SSD_PB, 1), 0)
    y_rows = jnp.zeros((SSD_PB, NB), F32)
    for pp in range(SSD_PB):
        h_new = ela * ss_ref[0, pp] + xrows[pp:pp + 1, :] * btg
        ss_out[0, pp] = h_new
        y_rows = jnp.where(p_iota == pp, jnp.sum(ctg * h_new, axis=0, keepdims=True), y_rows)
    yt_ref[prow, :] = y_rows

    blk = pl.ds(pl.multiple_of(i * sb, sb), sb)
    vblk = v_ref[blk, :]
    row_iota = lax.broadcasted_iota(jnp.int32, (NB, LANES), 0)
    sub_iota = lax.broadcasted_iota(jnp.int32, (sb, 1), 0)
    o_blk = jnp.zeros((sb, GLA_V), F32)
    for s in range(sb):
        sel = (row_iota == i * sb + s).astype(BF16)
        sel3 = jnp.concatenate([sel] * 3, axis=0)
        qcol = _dot(qt_ref[...], sel)
        kcol = _dot(kt_ref[...], sel)
        ecol = _dot(ebt_ref[...], sel3)
        o_parts = []
        for h in range(GLA_HEADS):
            hd = slice(h * GLA_DK, (h + 1) * GLA_DK)
            s_new = ecol[hd] * sg_ref[s, h] + kcol[hd] * vblk[s:s + 1, h * GLA_DV:(h + 1) * GLA_DV]
            sg_out[s, h] = s_new
            o_parts.append(jnp.sum(qcol[hd] * s_new, axis=0, keepdims=True))
        o_blk = jnp.where(sub_iota == s, jnp.concatenate(o_parts, axis=-1), o_blk)
    o_ref[blk, :] = o_blk

    @pl.when(i == pl.num_programs(0) - 1)
    def _():
        g = seg("g", GLA_V)
        o = _head_rmsnorm(o_ref[...], GLA_DV) * glag_ref[...] * (g * _sigmoid(g))
        z = seg("z", SSD_INNER)
        y = yt_ref[...].T + dsk_ref[...] * xs_ref[...]
        y = _head_rmsnorm(y * (z * _sigmoid(z)), SSD_INNER // SSD_GROUPS) * ssdg_ref[...]
        gate_a = jnp.concatenate([seg("ga0", GA_SPLIT), seg("ga1", D_MODEL - GA_SPLIT)], axis=-1)
        y_ref[:, 0, :] = _merge_out(x_ref[:, 0, :], o.astype(BF16), y.astype(BF16), gate_a, seg("gb", D_MODEL),
                                wbg_ref, wbs_ref, wout_ref, fg_ref)


def _sample_call(x, state_gla, state_ssm, state_conv, params, consts):
    NB = x.shape[0]
    sb = SAMPLE_SB
    steps = NB // sb
    assert NB == LANES and NB % sb == 0 and steps * SSD_PB == SSD_INNER
    assert x.shape == (NB, 1, D_MODEL)
    ss_t = jnp.transpose(state_ssm, (1, 2, 3, 0))
    sc_t = jnp.transpose(state_conv, (1, 0, 2))
    ins = [params[n] for n in _PARAM_ORDER]
    full = lambda a: pl.BlockSpec(a.shape, lambda i: (0,) * a.ndim)
    sg_spec = pl.BlockSpec((sb, GLA_HEADS, GLA_DK, GLA_DV), lambda i: (i, 0, 0, 0))
    pblocks = SSD_HEADDIM // SSD_PB
    ss_spec = pl.BlockSpec((1, SSD_PB, SSD_STATE, NB), lambda i: (i // pblocks, i % pblocks, 0, 0))
    out_shape = (
        jax.ShapeDtypeStruct(x.shape, F32),
        jax.ShapeDtypeStruct(state_gla.shape, F32),
        jax.ShapeDtypeStruct(ss_t.shape, F32),
        jax.ShapeDtypeStruct(sc_t.shape, F32),
    )
    y, sg, ss, sc = pl.pallas_call(
        functools.partial(_sample_kernel, NB),
        grid=(steps,),
        in_specs=[full(x), sg_spec, ss_spec, full(sc_t)] + [full(a) for a in ins],
        out_specs=(full(x), sg_spec, ss_spec, full(sc_t)),
        out_shape=out_shape,
        scratch_shapes=[
            pltpu.VMEM((NG, NB, GW), F32),
            pltpu.VMEM((NB, D_MODEL), BF16),
            pltpu.VMEM((GLA_QK, NB), BF16),
            pltpu.VMEM((GLA_QK, NB), BF16),
            pltpu.VMEM((GLA_QK, 3 * NB), BF16),
            pltpu.VMEM((SSD_INNER, NB), F32),
            pltpu.VMEM((LANES, NB), F32),
            pltpu.VMEM((NB, GLA_V), F32),
            pltpu.VMEM((NB, SSD_INNER), F32),
            pltpu.VMEM((SSD_BC, NB), F32),
            pltpu.VMEM((SSD_BC, NB), F32),
            pltpu.VMEM((NB, GLA_V), F32),
            pltpu.VMEM((SSD_INNER, NB), F32),
        ],
        compiler_params=pltpu.CompilerParams(dimension_semantics=("arbitrary",), vmem_limit_bytes=VMEM_LIMIT),
        name="decode_step",
    )(x, state_gla, ss_t, sc_t, *ins)
    return y, sg, jnp.transpose(ss, (3, 0, 1, 2)), jnp.transpose(sc, (1, 0, 2))


def kernel(x_prompt, x_sample, state_gla, state_ssm, state_conv, norm_g, w_in, w_a2, b_a2, gla_norm_g, conv_w, conv_b,
           dt_bias, a_log, d_skip, ssd_norm_g, w_br_gla, w_br_ssd, w_out, final_norm_g):
    params = _pack_params(norm_g[0], w_in[0], w_a2[0], b_a2[0], gla_norm_g[0], conv_w[0], conv_b[0], dt_bias[0],
                          a_log[0], d_skip[0], ssd_norm_g[0], w_br_gla[0], w_br_ssd[0], w_out[0], final_norm_g)
    consts = _consts()
    y_p, gla_p, ssm_p, conv_p = _prompt_call(x_prompt, params, consts)
    conv_p = conv_p[:, SUBLANES - (CONV_W - 1):, :]
    y_s, gla_s, ssm_s, conv_s = _sample_call(x_sample, state_gla[0], state_ssm[0], state_conv[0], params, consts)
    return (y_p, y_s, gla_p[None], ssm_p[None], conv_p[None], gla_s[None], ssm_s[None], conv_s[None])
```
